```python
import math
import numpy as np
import jax
import jax.numpy as jnp
from jax import lax

D_MODEL = 1024
BATCH = 8
SEQ = 4096
DEPTH = 2

GRID_W = 64
CTX_LEN = 256
EPS = 1e-6

F_GROUPS = 4
F_GROUP_DIM = 128
F_WIDTH = F_GROUPS * F_GROUP_DIM

DN_HEADS = 4
DN_DK = 128
DN_DV = 128
DN_WIDTH = DN_HEADS * DN_DV
DN_CONV = 3
DN_CHUNK = 64

MLA_HEADS = 8
MLA_Q_LORA = 384
MLA_KV_LORA = 256
MLA_NOPE = 64
MLA_ROPE = 32
MLA_V = 64
MLA_WIDTH = MLA_HEADS * MLA_V
ROPE_BASE = 10000.0
Q_BLOCK = 128

N_BRANCH = 3
BRANCH_WIDTH = 512

IN_WIDTHS = (F_WIDTH, F_WIDTH,
             DN_HEADS * DN_DK, DN_HEADS * DN_DK, DN_WIDTH, DN_WIDTH, 4 * DN_HEADS,
             MLA_Q_LORA, MLA_KV_LORA, MLA_ROPE, MLA_WIDTH,
             N_BRANCH * D_MODEL)
D_IN = 2 * F_WIDTH + 2 * DN_HEADS * DN_DK + 2 * DN_WIDTH + 4 * DN_HEADS + MLA_Q_LORA + MLA_KV_LORA + MLA_ROPE + MLA_WIDTH + N_BRANCH * D_MODEL

kernel_name = 'hybrid_fourier_deltanet_mla_diffusion_trunk'


def rmsnorm(x, g):
    xf = x.astype(jnp.float32)
    y = xf * lax.rsqrt(jnp.mean(xf * xf, axis=-1, keepdims=True) + EPS)
    return (y * g.astype(jnp.float32)).astype(x.dtype)


def l2norm(x):
    xf = x.astype(jnp.float32)
    return xf * lax.rsqrt(jnp.sum(xf * xf, axis=-1, keepdims=True) + EPS)


def split_in(p):
    return jnp.split(p, np.cumsum(IN_WIDTHS)[:-1].tolist(), axis=-1)


def fourier_mix(u, f_w):
    B, L, _ = u.shape
    ug = u.astype(jnp.float32).reshape(B, L, F_GROUPS, F_GROUP_DIM)
    spec = jnp.fft.fft2(ug, axes=(1, 3), norm='ortho').real.astype(u.dtype)
    out = jnp.einsum('blgc,gcd->blgd', spec, f_w)
    return out.reshape(B, L, F_WIDTH)


def depthwise_conv(u, w):
    K, C = w.shape
    pad = K // 2
    return lax.conv_general_dilated(u, w[:, None, :].astype(u.dtype), window_strides=(1,),
                                    padding=[(pad, K - 1 - pad)],
                                    dimension_numbers=('NWC', 'WIO', 'NWC'),
                                    feature_group_count=C)


def gdn_prepare(q, k, v, ab, conv_w, a_log, dt_bias):
    B, L, _ = q.shape
    qkv = jax.nn.silu(depthwise_conv(jnp.concatenate([q, k, v], axis=-1), conv_w))
    q, k, v = jnp.split(qkv, [DN_HEADS * DN_DK, 2 * DN_HEADS * DN_DK], axis=-1)
    q = l2norm(q.reshape(B, L, DN_HEADS, DN_DK)).transpose(0, 2, 1, 3)
    k = l2norm(k.reshape(B, L, DN_HEADS, DN_DK)).transpose(0, 2, 1, 3)
    v = v.reshape(B, L, DN_HEADS, DN_DV).transpose(0, 2, 1, 3)
    ab = ab.astype(jnp.float32).reshape(B, L, 4, DN_HEADS).transpose(2, 0, 3, 1)
    a_log = a_log.astype(jnp.float32)
    dt_bias = dt_bias.astype(jnp.float32)
    g = -jnp.exp(a_log)[:, None, :, None] * jax.nn.softplus(ab[:2] + dt_bias[:, None, :, None])
    beta = jax.nn.sigmoid(ab[2:])
    return q, k, v, g, beta


def gdn_chunk_scan(q, k, v, g, beta, s0):
    B, H, L, dk = q.shape
    dv = v.shape[-1]
    C = DN_CHUNK
    n = L // C
    f32 = jnp.float32
    out_dtype = v.dtype
    q = q.astype(f32).reshape(B, H, n, C, dk) * (dk ** -0.5)
    k = k.astype(f32).reshape(B, H, n, C, dk)
    v = v.astype(f32).reshape(B, H, n, C, dv)
    beta = beta.astype(f32).reshape(B, H, n, C)
    G = jnp.cumsum(g.astype(f32).reshape(B, H, n, C), axis=-1)
    incl = jnp.tril(jnp.ones((C, C), dtype=bool))
    strict = jnp.tril(jnp.ones((C, C), dtype=bool), -1)
    diff = G[..., :, None] - G[..., None, :]
    decay = jnp.where(incl, jnp.exp(jnp.where(incl, diff, 0.0)), 0.0)
    kk = jnp.einsum('bhnid,bhnjd->bhnij', k, k)
    a_mat = jnp.eye(C, dtype=f32) + jnp.where(strict, beta[..., :, None] * kk * decay, 0.0)
    rhs = jnp.concatenate([(beta * jnp.exp(G))[..., None] * k, beta[..., None] * v], axis=-1)
    sol = lax.linalg.triangular_solve(a_mat, rhs, left_side=True, lower=True, unit_diagonal=True)
    w_c, u_c = sol[..., :dk], sol[..., dk:]
    qk = jnp.einsum('bhnid,bhnjd->bhnij', q, k) * decay
    q_g = q * jnp.exp(G)[..., None]
    k_d = k * jnp.exp(G[..., -1:] - G)[..., None]
    g_last = jnp.exp(G[..., -1])

    def step(S, inp):
        w_n, u_n, qg_n, qk_n, kd_n, gl_n = inp
        u_new = u_n - jnp.einsum('bhcd,bhde->bhce', w_n, S)
        o = jnp.einsum('bhcd,bhde->bhce', qg_n, S) + jnp.einsum('bhcj,bhje->bhce', qk_n, u_new)
        S = S * gl_n[..., None, None] + jnp.einsum('bhcd,bhce->bhde', kd_n, u_new)
        return S, o

    xs = tuple(jnp.moveaxis(t, 2, 0) for t in (w_c, u_c, q_g, qk, k_d, g_last))
    s_final, o = lax.scan(step, s0.astype(f32), xs)
    o = jnp.moveaxis(o, 0, 2).reshape(B, H, L, dv)
    return o.astype(out_dtype), s_final


def gdn_direction(lat, ctx, reverse):
    if reverse:
        lat = tuple(jnp.flip(t, axis=2) for t in lat)
        ctx = tuple(jnp.flip(t, axis=2) for t in ctx)
    B, H = lat[0].shape[:2]
    s0 = jnp.zeros((B, H, DN_DK, DN_DV), jnp.float32)
    o_ctx, s_ctx = gdn_chunk_scan(*ctx, s0)
    o_lat, _ = gdn_chunk_scan(*lat, s_ctx)
    if reverse:
        o_lat = jnp.flip(o_lat, axis=2)
        o_ctx = jnp.flip(o_ctx, axis=2)
    return o_lat, o_ctx


def gdn_output(o, norm_w, z):
    B, H, L, dv = o.shape
    o = rmsnorm(o.transpose(0, 2, 1, 3).astype(z.dtype), norm_w).reshape(B, L, H * dv)
    return o * jax.nn.silu(z)


def axial_rope_tables(L):
    rows = L // GRID_W
    row = jnp.repeat(jnp.arange(rows, dtype=jnp.float32), GRID_W)
    col = jnp.tile(jnp.arange(GRID_W, dtype=jnp.float32), rows)
    n_freq = MLA_ROPE // 4
    inv = ROPE_BASE ** (-jnp.arange(n_freq, dtype=jnp.float32) / n_freq)
    ang_r = row[:, None] * inv
    ang_c = col[:, None] * inv
    ang = jnp.concatenate([ang_r, ang_r, ang_c, ang_c], axis=-1)
    return jnp.cos(ang), jnp.sin(ang)


def rotate_axial(x):
    x0, x1, x2, x3 = jnp.split(x, 4, axis=-1)
    return jnp.concatenate([-x1, x0, -x3, x2], axis=-1)


def apply_rope(x, cos, sin):
    return (x * cos + rotate_axial(x) * sin).astype(x.dtype)


def mla_qkv(cq, ckv, kpe, q_norm, w_uq, kv_norm, w_ukv, rope):
    B, L, _ = cq.shape
    q = (rmsnorm(cq, q_norm) @ w_uq).reshape(B, L, MLA_HEADS, MLA_NOPE + MLA_ROPE)
    kv = (rmsnorm(ckv, kv_norm) @ w_ukv).reshape(B, L, MLA_HEADS, MLA_NOPE + MLA_V)
    q_nope, q_pe = q[..., :MLA_NOPE], q[..., MLA_NOPE:]
    k_nope, v = kv[..., :MLA_NOPE], kv[..., MLA_NOPE:]
    if rope is not None:
        cos, sin = rope
        q_pe = apply_rope(q_pe, cos[:, None, :], sin[:, None, :])
        kpe = apply_rope(kpe, cos, sin)
    k_pe = jnp.broadcast_to(kpe[:, :, None, :], (B, L, MLA_HEADS, MLA_ROPE)).astype(k_nope.dtype)
    q = jnp.concatenate([q_nope, q_pe.astype(q_nope.dtype)], axis=-1)
    k = jnp.concatenate([k_nope, k_pe], axis=-1)
    return q, k, v


def attend(q, k, v):
    B, L, H, dqk = q.shape
    nb = L // Q_BLOCK
    scale = dqk ** -0.5
    qb = jnp.moveaxis(q.reshape(B, nb, Q_BLOCK, H, dqk), 1, 0)

    def block(qi):
        s = jnp.einsum('bqhd,bkhd->bhqk', qi, k).astype(jnp.float32) * scale
        p = jax.nn.softmax(s, axis=-1).astype(v.dtype)
        return jnp.einsum('bhqk,bkhd->bqhd', p, v)

    o = lax.map(block, qb)
    return jnp.moveaxis(o, 0, 1).reshape(B, L, H * v.shape[-1])


def merge_branches(y_a, y_b, y_c, gate_logits, w_branch, w_out):
    g_a, g_b, g_c = jnp.split(gate_logits, N_BRANCH, axis=-1)
    merged = (jax.nn.sigmoid(g_a) * (y_a @ w_branch[0])
              + jax.nn.sigmoid(g_b) * (y_b @ w_branch[1])
              + jax.nn.sigmoid(g_c) * (y_c @ w_branch[2]))
    return merged @ w_out


def mixer(h, hc, w_in, f_w, dn_conv, dn_a_log, dn_dt_bias, dn_norm,
          mla_q_norm, mla_w_uq, mla_kv_norm, mla_w_ukv, w_branch, w_out, need_ctx_out):
    B, L, _ = h.shape
    Lc = hc.shape[1]
    (fa_x, fa_z, dn_q, dn_k, dn_v, dn_z, dn_ab, cq, ckv, kpe, mla_z, gate_logits) = split_in(h @ w_in)
    (fa_xc, fa_zc, dn_qc, dn_kc, dn_vc, dn_zc, dn_abc, cqc, ckvc, kpec, mla_zc, gate_logits_c) = split_in(hc @ w_in)

    y_a = fourier_mix(fa_x, f_w) * jax.nn.silu(fa_z)

    q, k, v, g, beta = gdn_prepare(dn_q, dn_k, dn_v, dn_ab, dn_conv, dn_a_log, dn_dt_bias)
    qc, kc, vc, gc, betac = gdn_prepare(dn_qc, dn_kc, dn_vc, dn_abc, dn_conv, dn_a_log, dn_dt_bias)
    o_f, oc_f = gdn_direction((q, k, v, g[0], beta[0]), (qc, kc, vc, gc[0], betac[0]), False)
    o_b, oc_b = gdn_direction((q, k, v, g[1], beta[1]), (qc, kc, vc, gc[1], betac[1]), True)
    y_b = gdn_output(o_f + o_b, dn_norm, dn_z)

    rope = axial_rope_tables(L)
    q_m, k_m, v_m = mla_qkv(cq, ckv, kpe, mla_q_norm, mla_w_uq, mla_kv_norm, mla_w_ukv, rope)
    qc_m, kc_m, vc_m = mla_qkv(cqc, ckvc, kpec, mla_q_norm, mla_w_uq, mla_kv_norm, mla_w_ukv, None)
    o_m = attend(q_m, jnp.concatenate([k_m, kc_m], axis=1), jnp.concatenate([v_m, vc_m], axis=1))
    y_c = o_m * jax.nn.silu(mla_z)

    y = merge_branches(y_a, y_b, y_c, gate_logits, w_branch, w_out)

    yc = None
    if need_ctx_out:
        yc_a = fourier_mix(fa_xc, f_w) * jax.nn.silu(fa_zc)
        yc_b = gdn_output(oc_f + oc_b, dn_norm, dn_zc)
        yc_c = attend(qc_m, kc_m, vc_m) * jax.nn.silu(mla_zc)
        yc = merge_branches(yc_a, yc_b, yc_c, gate_logits_c, w_branch, w_out)
    return y, yc


def trunk_layer(x, ctx, c, c_ctx, w_mod, b_mod, g_pre, g_post, w_in, f_w, dn_conv, dn_a_log, dn_dt_bias,
                dn_norm, mla_q_norm, mla_w_uq, mla_kv_norm, mla_w_ukv, w_branch, w_out, need_ctx_out):
    mod = jax.nn.silu(c) @ w_mod + b_mod
    shift, scale, gate = jnp.split(mod[:, None, :], 3, axis=-1)
    mod_c = jax.nn.silu(c_ctx) @ w_mod + b_mod
    shift_c, scale_c, gate_c = jnp.split(mod_c, 3, axis=-1)
    h = rmsnorm(x, g_pre) * (1.0 + scale) + shift
    hc = rmsnorm(ctx, g_pre) * (1.0 + scale_c) + shift_c
    y, yc = mixer(h, hc, w_in, f_w, dn_conv, dn_a_log, dn_dt_bias, dn_norm,
                  mla_q_norm, mla_w_uq, mla_kv_norm, mla_w_ukv, w_branch, w_out, need_ctx_out)
    x = x + gate * rmsnorm(y, g_post)
    if need_ctx_out:
        ctx = ctx + gate_c * rmsnorm(yc, g_post)
    return x, ctx


def setup_inputs(seed: int = 0) -> dict:
    key = jax.random.key(seed)
    ks = jax.random.split(key, 24)
    f32 = jnp.float32

    def nrm(k, shape, fan_in):
        return jax.random.normal(k, shape, f32) * (fan_in ** -0.5)

    def gain(k, shape):
        return 1.0 + 0.02 * jax.random.normal(k, shape, f32)

    dt = jnp.exp(jax.random.uniform(ks[10], (DEPTH, 2, DN_HEADS), f32, math.log(1e-3), math.log(1e-1)))
    dn_dt_bias = dt + jnp.log(-jnp.expm1(-dt))
    dn_a_log = jnp.log(jax.random.uniform(ks[11], (DEPTH, 2, DN_HEADS), f32, 1.0, 16.0))
    return {
        'x': jax.random.normal(ks[0], (BATCH, SEQ, D_MODEL), f32),
        'c': jax.random.normal(ks[1], (BATCH, D_MODEL), f32),
        'ctx': jax.random.normal(ks[2], (BATCH, CTX_LEN, D_MODEL), f32),
        'c_ctx': jax.random.normal(ks[3], (D_MODEL,), f32),
        'w_mod': nrm(ks[4], (DEPTH, D_MODEL, 3 * D_MODEL), D_MODEL),
        'b_mod': 0.02 * jax.random.normal(ks[5], (DEPTH, 3 * D_MODEL), f32),
        'g_pre': gain(ks[6], (DEPTH, D_MODEL)),
        'g_post': gain(ks[7], (DEPTH, D_MODEL)),
        'w_in': nrm(ks[8], (DEPTH, D_MODEL, D_IN), D_MODEL),
        'f_w': nrm(ks[9], (DEPTH, F_GROUPS, F_GROUP_DIM, F_GROUP_DIM), F_GROUP_DIM),
        'dn_conv': nrm(ks[12], (DEPTH, DN_CONV, 2 * DN_HEADS * DN_DK + DN_WIDTH), DN_CONV),
        'dn_a_log': dn_a_log,
        'dn_dt_bias': dn_dt_bias,
        'dn_norm': gain(ks[13], (DEPTH, DN_DV)),
        'mla_q_norm': gain(ks[14], (DEPTH, MLA_Q_LORA)),
        'mla_w_uq': nrm(ks[15], (DEPTH, MLA_Q_LORA, MLA_HEADS * (MLA_NOPE + MLA_ROPE)), MLA_Q_LORA),
        'mla_kv_norm': gain(ks[16], (DEPTH, MLA_KV_LORA)),
        'mla_w_ukv': nrm(ks[17], (DEPTH, MLA_KV_LORA, MLA_HEADS * (MLA_NOPE + MLA_V)), MLA_KV_LORA),
        'w_branch': nrm(ks[18], (DEPTH, N_BRANCH, BRANCH_WIDTH, D_MODEL), BRANCH_WIDTH),
        'w_out': nrm(ks[19], (DEPTH, D_MODEL, D_MODEL), D_MODEL),
    }


def reference(x, c, ctx, c_ctx, w_mod, b_mod, g_pre, g_post, w_in, f_w, dn_conv, dn_a_log, dn_dt_bias,
              dn_norm, mla_q_norm, mla_w_uq, mla_kv_norm, mla_w_ukv, w_branch, w_out):
    for l in range(DEPTH):
        x, ctx = trunk_layer(x, ctx, c, c_ctx, w_mod[l], b_mod[l], g_pre[l], g_post[l], w_in[l], f_w[l],
                             dn_conv[l], dn_a_log[l], dn_dt_bias[l], dn_norm[l], mla_q_norm[l], mla_w_uq[l],
                             mla_kv_norm[l], mla_w_ukv[l], w_branch[l], w_out[l],
                             need_ctx_out=(l < DEPTH - 1))
    return x
```

```python
import functools
import math

import jax
import jax.numpy as jnp
from jax import lax
from jax.experimental import pallas as pl
from jax.experimental.pallas import tpu as pltpu

F32 = jnp.float32
BF = jnp.bfloat16
HIGHEST = lax.Precision.HIGHEST

D = 1024
EPS = 1e-6
GRID_W = 64
F_GROUPS = 4
F_GD = 128
F_W = 512
DN_H = 4
DN_DK = 128
DN_W = 512
DN_CHUNK = 64
M_H = 8
M_QL = 384
M_KVL = 256
M_NOPE = 64
M_ROPE = 32
M_V = 64
M_HP = 128
ROPE_BASE = 10000.0
LANE = 128
VMEM_LIMIT = 56 * 1024 * 1024

_OFF = {}
_o = 0
for _n, _w in (("fa_x", 512), ("fa_z", 512), ("dn_q", 512), ("dn_k", 512), ("dn_v", 512), ("dn_z", 512),
               ("dn_ab", 16), ("cq", M_QL), ("ckv", M_KVL), ("kpe", M_ROPE), ("mla_z", 512), ("gate", 3 * D)):
    _OFF[_n] = (_o, _o + _w)
    _o += _w


def _tile(n, pref, mult=16):
    t = min(n, pref)
    while t > mult and (n % t or t % mult):
        t -= mult
    assert n % t == 0, (n, pref)
    return t


def _params(sem):
    return pltpu.CompilerParams(dimension_semantics=sem, vmem_limit_bytes=VMEM_LIMIT)


def _silu(v):
    return v * jax.nn.sigmoid(v)


def _dot(a, b):
    return jnp.dot(a.astype(BF), b.astype(BF), preferred_element_type=F32)


def _mod_kernel(c_ref, w_ref, b_ref, o_ref):
    c = c_ref[...]
    o_ref[...] = jnp.dot(_silu(c), w_ref[...], precision=HIGHEST, preferred_element_type=F32) + b_ref[...]


def _modulation(c_all, w_mod, b_mod):
    r = c_all.shape[0]
    tn = 512
    return pl.pallas_call(
        _mod_kernel,
        grid=(3 * D // tn,),
        in_specs=[pl.BlockSpec((r, D), lambda n: (0, 0)),
                  pl.BlockSpec((D, tn), lambda n: (0, n)),
                  pl.BlockSpec((1, tn), lambda n: (0, n))],
        out_specs=pl.BlockSpec((r, tn), lambda n: (0, n)),
        out_shape=jax.ShapeDtypeStruct((r, 3 * D), F32),
        compiler_params=_params(("arbitrary",)),
        name="modulation",
    )(c_all, w_mod, b_mod.reshape(1, 3 * D))


def _inproj_kernel(x_ref, sc_ref, sh_ref, g_ref, *refs, n_w):
    w_refs, o_refs = refs[:n_w], refs[n_w:]
    x = x_ref[0]
    y = x * lax.rsqrt(jnp.mean(x * x, axis=-1, keepdims=True) + EPS) * g_ref[...]
    hb = (y * (1.0 + sc_ref[0]) + sh_ref[0]).astype(BF)
    for w_ref, o_ref in zip(w_refs, o_refs):
        n = w_ref.shape[1]
        step = 512 if n % 512 == 0 else n
        for c0 in range(0, n, step):
            o_ref[0, :, c0:c0 + step] = jnp.dot(
                hb, w_ref[:, c0:c0 + step], preferred_element_type=F32).astype(o_ref.dtype)


def _inproj(x, scale, shift, g_pre, weights, out_dtypes, per_batch_mod):
    b, l, _ = x.shape
    tm = _tile(l, 512)
    mod_map = (lambda bi, i: (bi, 0, 0)) if per_batch_mod else (lambda bi, i: (0, 0, 0))
    in_specs = [pl.BlockSpec((1, tm, D), lambda bi, i: (bi, i, 0)),
                pl.BlockSpec((1, 1, D), mod_map),
                pl.BlockSpec((1, 1, D), mod_map),
                pl.BlockSpec((1, D), lambda bi, i: (0, 0))]
    in_specs += [pl.BlockSpec(w.shape, lambda bi, i: (0, 0)) for w in weights]
    out_specs = [pl.BlockSpec((1, tm, w.shape[1]), lambda bi, i: (bi, i, 0)) for w in weights]
    out_shape = [jax.ShapeDtypeStruct((b, l, w.shape[1]), dt) for w, dt in zip(weights, out_dtypes)]
    return pl.pallas_call(
        functools.partial(_inproj_kernel, n_w=len(weights)),
        grid=(b, l // tm),
        in_specs=in_specs, out_specs=out_specs, out_shape=out_shape,
        compiler_params=_params(("parallel", "parallel")),
        name="inproj",
    )(x, scale, shift, g_pre.reshape(1, D), *weights)


def _dft_kernel(c_ref, s_ref, x_ref, o_ref, accp, accq):
    k = pl.program_id(2)

    @pl.when(k == 0)
    def _():
        accp[...] = jnp.zeros_like(accp)
        accq[...] = jnp.zeros_like(accq)

    xk = x_ref[0]
    accp[...] += jnp.dot(c_ref[...], xk, preferred_element_type=F32)
    accq[...] += jnp.dot(s_ref[...], xk, preferred_element_type=F32)

    @pl.when(k == pl.num_programs(2) - 1)
    def _():
        o_ref[0, :, :F_W] = accp[...].astype(o_ref.dtype)
        o_ref[0, :, F_W:] = accq[...].astype(o_ref.dtype)


def _dft_tables(l):
    blk = min(l, 64)
    j1 = jnp.arange(l // blk, dtype=jnp.int32)[:, None] * blk
    j2 = jnp.arange(blk, dtype=jnp.int32)[:, None]
    k = jnp.arange(l, dtype=jnp.int32)[None, :]
    w = 2.0 * math.pi / l
    a = ((j1 * k) % l).astype(F32) * w
    bb = ((j2 * k) % l).astype(F32) * w
    ca, sa, cb, sb = jnp.cos(a), jnp.sin(a), jnp.cos(bb), jnp.sin(bb)
    nrm = l ** -0.5
    cos = (ca[:, None, :] * cb[None] - sa[:, None, :] * sb[None]).reshape(l, l) * nrm
    sin = (sa[:, None, :] * cb[None] + ca[:, None, :] * sb[None]).reshape(l, l) * nrm
    return cos.astype(BF), sin.astype(BF)


def _dft(fa_x, cos_t, sin_t):
    b, l, _ = fa_x.shape
    tm = _tile(l, 1024)
    tk = _tile(l, 1024)
    return pl.pallas_call(
        _dft_kernel,
        grid=(b, l // tm, l // tk),
        in_specs=[pl.BlockSpec((tm, tk), lambda bi, i, k: (i, k)),
                  pl.BlockSpec((tm, tk), lambda bi, i, k: (i, k)),
                  pl.BlockSpec((1, tk, F_W), lambda bi, i, k: (bi, k, 0))],
        out_specs=pl.BlockSpec((1, tm, 2 * F_W), lambda bi, i, k: (bi, i, 0)),
        out_shape=jax.ShapeDtypeStruct((b, l, 2 * F_W), BF),
        scratch_shapes=[pltpu.VMEM((tm, F_W), F32), pltpu.VMEM((tm, F_W), F32)],
        compiler_params=_params(("parallel", "parallel", "arbitrary")),
        name="dft",
    )(cos_t, sin_t, fa_x)


def _gdn_prep_kernel(x_ref, prev_ref, next_ref, ab_ref, cw_ref, alog_ref, dtb_ref,
                     q_ref, k_ref, v_ref, g_ref):
    i = pl.program_id(1)
    last = pl.num_programs(1) - 1
    x = x_ref[0].astype(F32)
    tm = x.shape[0]
    hr = prev_ref.shape[1]
    prev_row = prev_ref[0, hr - 1:hr, :].astype(F32) * (i > 0).astype(F32)
    next_row = next_ref[0, 0:1, :].astype(F32) * (i < last).astype(F32)
    rows = lax.broadcasted_iota(jnp.int32, x.shape, 0)
    x_dn = jnp.where(rows == 0, prev_row, pltpu.roll(x, 1, axis=0))
    x_up = jnp.where(rows == tm - 1, next_row, pltpu.roll(x, tm - 1, axis=0))
    cw = cw_ref[...]
    y = _silu(cw[0:1, :] * x_dn + cw[1:2, :] * x + cw[2:3, :] * x_up)
    for h in range(DN_H):
        qs = y[:, h * DN_DK:(h + 1) * DN_DK]
        ks = y[:, DN_W + h * DN_DK:DN_W + (h + 1) * DN_DK]
        qn = qs * lax.rsqrt(jnp.sum(qs * qs, axis=-1, keepdims=True) + EPS) * (DN_DK ** -0.5)
        kn = ks * lax.rsqrt(jnp.sum(ks * ks, axis=-1, keepdims=True) + EPS)
        q_ref[0, :, h * DN_DK:(h + 1) * DN_DK] = qn.astype(q_ref.dtype)
        k_ref[0, :, h * DN_DK:(h + 1) * DN_DK] = kn.astype(k_ref.dtype)
    v_ref[0] = y[:, 2 * DN_W:].astype(v_ref.dtype)
    a = ab_ref[0]
    cols = lax.broadcasted_iota(jnp.int32, a.shape, 1)
    z = a + dtb_ref[...]
    softplus = jnp.maximum(z, 0.0) + jnp.log1p(jnp.exp(-jnp.abs(z)))
    g = -jnp.exp(alog_ref[...]) * softplus
    g_ref[0] = jnp.where(cols < 2 * DN_H, g, jnp.where(cols < 4 * DN_H, jax.nn.sigmoid(a), 0.0))


def _gdn_prep(qkv, ab, conv_w, a_log, dt_bias):
    b, l, w = qkv.shape
    tm = _tile(l, 512)
    hr = 16
    nb = tm // hr
    pad = jnp.zeros((1, LANE - 2 * DN_H), F32)
    alog = jnp.concatenate([a_log.reshape(1, 2 * DN_H).astype(F32), pad], axis=1)
    dtb = jnp.concatenate([dt_bias.reshape(1, 2 * DN_H).astype(F32), pad], axis=1)
    last_blk = l // hr - 1
    outs = pl.pallas_call(
        _gdn_prep_kernel,
        grid=(b, l // tm),
        in_specs=[pl.BlockSpec((1, tm, w), lambda bi, i: (bi, i, 0)),
                  pl.BlockSpec((1, hr, w), lambda bi, i: (bi, jnp.maximum(i * nb - 1, 0), 0)),
                  pl.BlockSpec((1, hr, w), lambda bi, i: (bi, jnp.minimum((i + 1) * nb, last_blk), 0)),
                  pl.BlockSpec((1, tm, LANE), lambda bi, i: (bi, i, 0)),
                  pl.BlockSpec((3, w), lambda bi, i: (0, 0)),
                  pl.BlockSpec((1, LANE), lambda bi, i: (0, 0)),
                  pl.BlockSpec((1, LANE), lambda bi, i: (0, 0))],
        out_specs=[pl.BlockSpec((1, tm, DN_W), lambda bi, i: (bi, i, 0)),
                   pl.BlockSpec((1, tm, DN_W), lambda bi, i: (bi, i, 0)),
                   pl.BlockSpec((1, tm, DN_W), lambda bi, i: (bi, i, 0)),
                   pl.BlockSpec((1, tm, LANE), lambda bi, i: (bi, i, 0))],
        out_shape=[jax.ShapeDtypeStruct((b, l, DN_W), BF)] * 3 + [jax.ShapeDtypeStruct((b, l, LANE), F32)],
        compiler_params=_params(("parallel", "parallel")),
        name="gdn_prep",
    )(qkv, qkv, qkv, ab, conv_w.astype(F32), alog, dtb)
    return outs


def _gdn_scan_kernel(qf, kf, vf, gf, qb, kb, vb, gb, s0_ref, of_ref, ob_ref, s_ref):
    j = pl.program_id(1)

    @pl.when(j == 0)
    def _():
        s_ref[...] = s0_ref[...]

    c = DN_CHUNK
    ri = lax.broadcasted_iota(jnp.int32, (c, c), 0)
    ci = lax.broadcasted_iota(jnp.int32, (c, c), 1)
    eye = (ri == ci).astype(F32)
    lvl_masks = []
    bsz = 1
    while bsz < c:
        lvl_masks.append((ri // (2 * bsz) == ci // (2 * bsz)) & (ri // bsz != ci // bsz))
        bsz *= 2
    for d,(q_r, k_r, v_r, g_r, o_r) in enumerate(((qf, kf, vf, gf, of_ref), (qb, kb, vb, gb, ob_ref))):
        incl = (ri >= ci) if d == 0 else (ri <= ci)
        strict = (ri > ci) if d == 0 else (ri < ci)
        last = c - 1 if d == 0 else 0
        ga = g_r[0]
        gcum = jnp.dot(incl.astype(F32), ga, precision=HIGHEST, preferred_element_type=F32)
        gcum_t = jnp.concatenate([gcum, jnp.zeros_like(gcum)], axis=0).T
        for h in range(DN_H):
            col = d * DN_H + h
            gc = gcum[:, col:col + 1]
            gr = gcum_t[col:col + 1, :c]
            bc = ga[:, 2 * DN_H + col:2 * DN_H + col + 1]
            gl = gcum[last:last + 1, col:col + 1]
            q = q_r[0, :, h * DN_DK:(h + 1) * DN_DK].astype(F32)
            k = k_r[0, :, h * DN_DK:(h + 1) * DN_DK].astype(F32)
            v = v_r[0, :, h * DN_DK:(h + 1) * DN_DK].astype(F32)
            eg = jnp.exp(gc)
            decay = jnp.where(incl, jnp.exp(jnp.where(incl, gc - gr, 0.0)), 0.0)
            kb16 = k.astype(BF)
            kk = lax.dot_general(kb16, kb16, (((1,), (1,)), ((), ())), preferred_element_type=F32)
            qk = lax.dot_general(q.astype(BF), kb16, (((1,), (1,)), ((), ())), preferred_element_type=F32) * decay
            a_m = jnp.where(strict, bc * kk * decay, 0.0)
            t = eye - jnp.where(lvl_masks[0], a_m, 0.0)
            for lm in lvl_masks[1:]:
                t = t - _dot(t, _dot(jnp.where(lm, a_m, 0.0), t))
            rhs = jnp.concatenate([(bc * eg) * k, bc * v], axis=1)
            sol = _dot(t, rhs)
            w_c, u_c = sol[:, :DN_DK], sol[:, DN_DK:]
            s = s_ref[0, d, h]
            u_new = u_c - _dot(w_c, s)
            o = _dot(q * eg, s) + _dot(qk, u_new)
            o_r[0, :, h * DN_DK:(h + 1) * DN_DK] = o.astype(o_r.dtype)
            kd = (k * jnp.exp(gl - gc)).astype(BF)
            s_ref[0, d, h] = s * jnp.exp(gl) + lax.dot_general(
                kd, u_new.astype(BF), (((0,), (0,)), ((), ())), preferred_element_type=F32)


def _gdn_scan(qn, kn, v, gact, s0):
    b, l, _ = qn.shape
    n = l // DN_CHUNK
    fwd = lambda bi, j: (bi, j, 0)
    bwd = lambda bi, j: (bi, n - 1 - j, 0)
    st = lambda bi, j: (bi, 0, 0, 0, 0)
    blk = lambda w, m: pl.BlockSpec((1, DN_CHUNK, w), m)
    return pl.pallas_call(
        _gdn_scan_kernel,
        grid=(b, n),
        in_specs=[blk(DN_W, fwd), blk(DN_W, fwd), blk(DN_W, fwd), blk(LANE, fwd),
                  blk(DN_W, bwd), blk(DN_W, bwd), blk(DN_W, bwd), blk(LANE, bwd),
                  pl.BlockSpec((1, 2, DN_H, DN_DK, DN_DK), st)],
        out_specs=[blk(DN_W, fwd), blk(DN_W, bwd), pl.BlockSpec((1, 2, DN_H, DN_DK, DN_DK), st)],
        out_shape=[jax.ShapeDtypeStruct((b, l, DN_W), BF), jax.ShapeDtypeStruct((b, l, DN_W), BF),
                   jax.ShapeDtypeStruct((b, 2, DN_H, DN_DK, DN_DK), F32)],
        compiler_params=_params(("parallel", "arbitrary")),
        name="gdn_scan",
    )(qn, kn, v, gact, qn, kn, v, gact, s0)


def _mla_prep_kernel(m_ref, qn_ref, kvn_ref, wq_ref, wqr_ref, wk_ref, wv_ref, cos_ref, sin_ref,
                     q_ref, k_ref, v_ref):
    m = m_ref[0].astype(F32)
    cq = m[:, :M_QL]
    ckv = m[:, M_QL:M_QL + M_KVL]
    kpa = m[:, M_QL + M_KVL:M_QL + M_KVL + LANE]
    kpb = m[:, M_QL + M_KVL + LANE:]
    hq = (cq * lax.rsqrt(jnp.mean(cq * cq, axis=-1, keepdims=True) + EPS) * qn_ref[...]).astype(BF)
    hkv = (ckv * lax.rsqrt(jnp.mean(ckv * ckv, axis=-1, keepdims=True) + EPS) * kvn_ref[...]).astype(BF)
    cos = cos_ref[...]
    sin = sin_ref[...]
    scale = (M_NOPE + M_ROPE) ** -0.5
    q_all = jnp.dot(hq, wq_ref[...], preferred_element_type=F32)
    q_rot = jnp.dot(hq, wqr_ref[...], preferred_element_type=F32)
    k_all = jnp.dot(hkv, wk_ref[...], preferred_element_type=F32)
    v_all = jnp.dot(hkv, wv_ref[...], preferred_element_type=F32)
    kpe = kpa * cos + kpb * sin
    lanes = lax.broadcasted_iota(jnp.int32, kpe.shape, 1)
    one_col = jnp.where(lanes == M_V, 1.0, 0.0)
    for h in range(M_H):
        sl = slice(h * M_HP, (h + 1) * M_HP)
        q_ref[0, h] = ((q_all[:, sl] * cos + q_rot[:, sl] * sin) * scale).astype(q_ref.dtype)
        k_ref[0, h] = (k_all[:, sl] + kpe).astype(k_ref.dtype)
        v_ref[0, h] = (v_all[:, sl] + one_col).astype(v_ref.dtype)


def _mla_prep(mla_in, q_norm, kv_norm, wq, wqr, wk, wv, cos_t, sin_t):
    b, l, w = mla_in.shape
    tm = _tile(l, 512)
    full = lambda a: pl.BlockSpec(a.shape, lambda bi, i: (0,) * a.ndim)
    hw = M_H * M_HP
    return pl.pallas_call(
        _mla_prep_kernel,
        grid=(b, l // tm),
        in_specs=[pl.BlockSpec((1, tm, w), lambda bi, i: (bi, i, 0)),
                  pl.BlockSpec((1, M_QL), lambda bi, i: (0, 0)),
                  pl.BlockSpec((1, M_KVL), lambda bi, i: (0, 0)),
                  full(wq), full(wqr), full(wk), full(wv),
                  pl.BlockSpec((tm, LANE), lambda bi, i: (i, 0)),
                  pl.BlockSpec((tm, LANE), lambda bi, i: (i, 0))],
        out_specs=[pl.BlockSpec((1, M_H, tm, M_HP), lambda bi, i: (bi, 0, i, 0))] * 3,
        out_shape=[jax.ShapeDtypeStruct((b, M_H, l, M_HP), BF)] * 3,
        compiler_params=_params(("parallel", "parallel")),
        name="mla_prep",
    )(mla_in, q_norm.reshape(1, M_QL).astype(F32), kv_norm.reshape(1, M_KVL).astype(F32),
      wq, wqr, wk, wv, cos_t, sin_t)


def _attn_kernel(q_ref, *refs, n_src):
    k_refs, v_refs, o_ref = refs[:n_src], refs[n_src:2 * n_src], refs[2 * n_src]
    outs = []
    for hh in range(2):
        q = q_ref[0, hh]
        s = [lax.dot_general(q, k_r[0, hh], (((1,), (1,)), ((), ())), preferred_element_type=F32)
             for k_r in k_refs]
        m = s[0].max(axis=-1, keepdims=True)
        for si in s[1:]:
            m = jnp.maximum(m, si.max(axis=-1, keepdims=True))
        acc = None
        for si, v_r in zip(s, v_refs):
            p = jnp.exp(si - m).astype(BF)
            part = jnp.dot(p, v_r[0, hh], preferred_element_type=F32)
            acc = part if acc is None else acc + part
        outs.append(acc[:, :M_V] / acc[:, M_V:M_V + 1])
    o_ref[0] = jnp.concatenate(outs, axis=1).astype(o_ref.dtype)


def _attention(q, ks, vs):
    b, _, l, _ = q.shape
    tq = _tile(l, 256)
    n_src = len(ks)
    kv_spec = lambda a: pl.BlockSpec((1, 2, a.shape[2], M_HP), lambda bi, hp, i: (bi, hp, 0, 0))
    return pl.pallas_call(
        functools.partial(_attn_kernel, n_src=n_src),
        grid=(b, M_H // 2, l // tq),
        in_specs=[pl.BlockSpec((1, 2, tq, M_HP), lambda bi, hp, i: (bi, hp, i, 0))]
                 + [kv_spec(a) for a in ks] + [kv_spec(a) for a in vs],
        out_specs=pl.BlockSpec((1, tq, 2 * M_V), lambda bi, hp, i: (bi, i, hp)),
        out_shape=jax.ShapeDtypeStruct((b, l, M_H * M_V), BF),
        compiler_params=_params(("parallel", "parallel", "arbitrary")),
        name="attention",
    )(q, *ks, *vs)


def _merge_kernel(pq_ref, faz_ref, of_ref, ob_ref, dnz_ref, om_ref, mz_ref, gl_ref, x_ref, gm_ref, gp_ref,
                  fw_ref, cc_ref, sc_ref, dnn_ref, wb_ref, wo_ref, o_ref):
    pq = pq_ref[0]
    faz = faz_ref[0].astype(F32)
    ya = []
    for g in range(F_GROUPS):
        sl = slice(g * F_GD, (g + 1) * F_GD)
        spec = (jnp.dot(pq[:, sl], cc_ref[...], preferred_element_type=F32)
                - jnp.dot(pq[:, F_W + g * F_GD:F_W + (g + 1) * F_GD], sc_ref[...], preferred_element_type=F32))
        t = jnp.dot(spec.astype(BF), fw_ref[g], preferred_element_type=F32)
        ya.append((t * _silu(faz[:, sl])).astype(BF))
    ya = jnp.concatenate(ya, axis=1)
    osum = of_ref[0].astype(F32) + ob_ref[0].astype(F32)
    dnz = dnz_ref[0].astype(F32)
    yb = []
    for h in range(DN_H):
        sl = slice(h * DN_DK, (h + 1) * DN_DK)
        seg = osum[:, sl]
        nrm = seg * lax.rsqrt(jnp.mean(seg * seg, axis=-1, keepdims=True) + EPS) * dnn_ref[...]
        yb.append((nrm * _silu(dnz[:, sl])).astype(BF))
    yb = jnp.concatenate(yb, axis=1)
    yc = (om_ref[0].astype(F32) * _silu(mz_ref[0].astype(F32))).astype(BF)
    merged = None
    for idx, yy in enumerate((ya, yb, yc)):
        gate = jax.nn.sigmoid(gl_ref[0, :, idx * D:(idx + 1) * D].astype(F32))
        term = gate * jnp.dot(yy, wb_ref[idx], preferred_element_type=F32)
        merged = term if merged is None else merged + term
    y = jnp.dot(merged.astype(BF), wo_ref[...], preferred_element_type=F32)
    yn = y * lax.rsqrt(jnp.mean(y * y, axis=-1, keepdims=True) + EPS) * gp_ref[...]
    o_ref[0] = x_ref[0] + gm_ref[0] * yn


def _merge(pq, fa_z, o_f, o_b, dn_z, o_m, mla_z, gate_logits, x, gate_mod, g_post, fw, cc, sc, dn_norm, wb, wo,
           per_batch_mod):
    b, l, _ = x.shape
    tm = _tile(l, 512)
    row = lambda w: pl.BlockSpec((1, tm, w), lambda bi, i: (bi, i, 0))
    full = lambda a: pl.BlockSpec(a.shape, lambda bi, i: (0,) * a.ndim)
    mod_map = (lambda bi, i: (bi, 0, 0)) if per_batch_mod else (lambda bi, i: (0, 0, 0))
    gp = g_post.reshape(1, D).astype(F32)
    dnn = dn_norm.reshape(1, DN_DK).astype(F32)
    return pl.pallas_call(
        _merge_kernel,
        grid=(b, l // tm),
        in_specs=[row(2 * F_W), row(F_W), row(DN_W), row(DN_W), row(DN_W), row(512), row(512), row(3 * D), row(D),
                  pl.BlockSpec((1, 1, D), mod_map), full(gp), full(fw), full(cc), full(sc), full(dnn),
                  full(wb), full(wo)],
        out_specs=row(D),
        out_shape=jax.ShapeDtypeStruct((b, l, D), F32),
        compiler_params=_params(("parallel", "parallel")),
        name="merge",
    )(pq, fa_z, o_f, o_b, dn_z, o_m, mla_z, gate_logits, x, gate_mod, gp, fw, cc, sc, dnn, wb, wo)


def _rot_cols(w):
    q = M_ROPE // 4
    return jnp.concatenate([-w[..., q:2 * q], w[..., :q], -w[..., 3 * q:], w[..., 2 * q:3 * q]], axis=-1)


def _layer_weights(w_in, mla_w_uq, mla_w_ukv):
    col = lambda n: w_in[:, _OFF[n][0]:_OFF[n][1]]
    zeros = lambda n: jnp.zeros((D, n), F32)
    w_qkv = jnp.concatenate([col("dn_q"), col("dn_k"), col("dn_v")], axis=1)
    w_ab = jnp.concatenate([col("dn_ab"), zeros(LANE - 4 * DN_H)], axis=1)
    kpe = col("kpe")
    rest = LANE - M_NOPE - M_ROPE
    kpe_a = jnp.concatenate([zeros(M_NOPE), kpe, zeros(rest)], axis=1)
    kpe_b = jnp.concatenate([zeros(M_NOPE), _rot_cols(kpe), zeros(rest)], axis=1)
    w_mla = jnp.concatenate([col("cq"), col("ckv"), kpe_a, kpe_b], axis=1)
    grp1 = [col("fa_x"), col("fa_z"), w_qkv, col("dn_z"), w_ab]
    grp2 = [w_mla, col("mla_z"), col("gate")]
    uq = mla_w_uq.reshape(M_QL, M_H, M_NOPE + M_ROPE)
    zq = jnp.zeros((M_QL, M_H, rest), F32)
    wq = jnp.concatenate([uq, zq], axis=-1).reshape(M_QL, M_H * M_HP)
    wqr = jnp.concatenate([jnp.zeros((M_QL, M_H, M_NOPE), F32), _rot_cols(uq[..., M_NOPE:]), zq],
                          axis=-1).reshape(M_QL, M_H * M_HP)
    ukv = mla_w_ukv.reshape(M_KVL, M_H, M_NOPE + M_V)
    zk = jnp.zeros((M_KVL, M_H, M_HP - M_NOPE), F32)
    wk = jnp.concatenate([ukv[..., :M_NOPE], zk], axis=-1).reshape(M_KVL, M_H * M_HP)
    wv = jnp.concatenate([ukv[..., M_NOPE:], jnp.zeros((M_KVL, M_H, M_HP - M_V), F32)],
                         axis=-1).reshape(M_KVL, M_H * M_HP)
    cast = lambda ws: [w.astype(BF) for w in ws]
    return cast(grp1), cast(grp2), cast([wq, wqr, wk, wv])


def _rope_tables(l, rotary):
    ones = jnp.ones((l, M_NOPE), F32)
    pad = jnp.zeros((l, LANE - M_NOPE - M_ROPE), F32)
    if not rotary:
        return (jnp.concatenate([ones, jnp.ones((l, M_ROPE), F32), pad], axis=1),
                jnp.zeros((l, LANE), F32))
    pos = jnp.arange(l, dtype=jnp.int32)
    row = (pos // GRID_W).astype(F32)
    colp = (pos % GRID_W).astype(F32)
    n_freq = M_ROPE // 4
    inv = ROPE_BASE ** (-jnp.arange(n_freq, dtype=F32) / n_freq)
    ang_r = row[:, None] * inv
    ang_c = colp[:, None] * inv
    ang = jnp.concatenate([ang_r, ang_r, ang_c, ang_c], axis=-1)
    return (jnp.concatenate([ones, jnp.cos(ang), pad], axis=1),
            jnp.concatenate([jnp.zeros((l, M_NOPE), F32), jnp.sin(ang), pad], axis=1))


def _chan_tables():
    j = jnp.arange(F_GD, dtype=jnp.int32)
    a = ((j[:, None] * j[None, :]) % F_GD).astype(F32) * (2.0 * math.pi / F_GD)
    nrm = F_GD ** -0.5
    return (jnp.cos(a) * nrm).astype(BF), (jnp.sin(a) * nrm).astype(BF)


def _layer(x, ctx, mod, p, tables, need_ctx_out):
    b = x.shape[0]
    shift, scale, gate = (mod[:b, None, i * D:(i + 1) * D] for i in range(3))
    shift_c, scale_c, gate_c = (mod[b:b + 1, None, i * D:(i + 1) * D] for i in range(3))
    grp1, grp2, (wq, wqr, wk, wv) = _layer_weights(p["w_in"], p["mla_w_uq"], p["mla_w_ukv"])
    dts1 = [BF, BF, BF, BF, F32]
    dts2 = [BF, BF, BF]

    def project(t, sc, sh, per_batch):
        a = _inproj(t, sc, sh, p["g_pre"], grp1, dts1, per_batch)
        bb = _inproj(t, sc, sh, p["g_pre"], grp2, dts2, per_batch)
        return a, bb

    (fa_x, fa_z, qkv, dn_z, ab), (mla_in, mla_z, gl) = project(x, scale, shift, True)
    (fa_xc, fa_zc, qkvc, dn_zc, abc), (mla_inc, mla_zc, glc) = project(ctx, scale_c, shift_c, False)

    qn, kn, vv, gact = _gdn_prep(qkv, ab, p["dn_conv"], p["dn_a_log"], p["dn_dt_bias"])
    qnc, knc, vvc, gactc = _gdn_prep(qkvc, abc, p["dn_conv"], p["dn_a_log"], p["dn_dt_bias"])
    s0 = jnp.zeros((b, 2, DN_H, DN_DK, DN_DK), F32)
    ofc, obc, s_ctx = _gdn_scan(qnc, knc, vvc, gactc, s0)
    of, ob, _ = _gdn_scan(qn, kn, vv, gact, s_ctx)

    q_m, k_m, v_m = _mla_prep(mla_in, p["mla_q_norm"], p["mla_kv_norm"], wq, wqr, wk, wv, *tables["rope"])
    qc_m, kc_m, vc_m = _mla_prep(mla_inc, p["mla_q_norm"], p["mla_kv_norm"], wq, wqr, wk, wv, *tables["rope_c"])
    o_m = _attention(q_m, [k_m, kc_m], [v_m, vc_m])

    pq = _dft(fa_x, *tables["dft"])
    fw = p["f_w"].astype(BF)
    wb = p["w_branch"].astype(BF)
    wo = p["w_out"].astype(BF)
    cc, sc = tables["chan"]
    x_new = _merge(pq, fa_z, of, ob, dn_z, o_m, mla_z, gl, x, gate, p["g_post"], fw, cc, sc, p["dn_norm"],
                   wb, wo, True)
    ctx_new = ctx
    if need_ctx_out:
        pqc = _dft(fa_xc, *tables["dft_c"])
        oc_m = _attention(qc_m, [kc_m], [vc_m])
        ctx_new = _merge(pqc, fa_zc, ofc, obc, dn_zc, oc_m, mla_zc, glc, ctx, gate_c, p["g_post"], fw, cc, sc,
                         p["dn_norm"], wb, wo, False)
    return x_new, ctx_new


def kernel(x, c, ctx, c_ctx, w_mod, b_mod, g_pre, g_post, w_in, f_w, dn_conv, dn_a_log, dn_dt_bias, dn_norm,
           mla_q_norm, mla_w_uq, mla_kv_norm, mla_w_ukv, w_branch, w_out):
    b, l, _ = x.shape
    lc = ctx.shape[1]
    depth = w_mod.shape[0]
    rows = -(-(b + 1) // 8) * 8
    c_all = jnp.concatenate([c, c_ctx[None, :], jnp.zeros((rows - b - 1, D), F32)], axis=0)
    tables = {"dft": _dft_tables(l), "dft_c": _dft_tables(lc), "chan": _chan_tables(),
              "rope": _rope_tables(l, True), "rope_c": _rope_tables(lc, False)}
    per_layer = dict(g_pre=g_pre, g_post=g_post, w_in=w_in, f_w=f_w, dn_conv=dn_conv, dn_a_log=dn_a_log,
                     dn_dt_bias=dn_dt_bias, dn_norm=dn_norm, mla_q_norm=mla_q_norm, mla_w_uq=mla_w_uq,
                     mla_kv_norm=mla_kv_norm, mla_w_ukv=mla_w_ukv, w_branch=w_branch, w_out=w_out)
    for li in range(depth):
        p = {k: v[li] for k, v in per_layer.items()}
        mod = _modulation(c_all, w_mod[li], b_mod[li])
        x, ctx = _layer(x, ctx, mod, p, tables, need_ctx_out=(li < depth - 1))
    return x
```

```python
import functools
import math

import jax
import jax.numpy as jnp
from jax import lax
from jax.experimental import pallas as pl
from jax.experimental.pallas import tpu as pltpu

F32 = jnp.float32
BF = jnp.bfloat16
HIGHEST = lax.Precision.HIGHEST

D = 1024
EPS = 1e-6
GRID_W = 64
F_GROUPS = 4
F_GD = 128
F_W = 512
DN_H = 4
DN_DK = 128
DN_W = 512
DN_CHUNK = 64
M_H = 8
M_QL = 384
M_KVL = 256
M_NOPE = 64
M_ROPE = 32
M_V = 64
M_HP = 128
ROPE_BASE = 10000.0
LANE = 128
VMEM_LIMIT = 56 * 1024 * 1024
ATT_SUB = 256

_OFF = {}
_o = 0
for _n, _w in (("fa_x", 512), ("fa_z", 512), ("dn_q", 512), ("dn_k", 512), ("dn_v", 512), ("dn_z", 512),
               ("dn_ab", 16), ("cq", M_QL), ("ckv", M_KVL), ("kpe", M_ROPE), ("mla_z", 512), ("gate", 3 * D)):
    _OFF[_n] = (_o, _o + _w)
    _o += _w


def _tile(n, pref, mult=16):
    t = min(n, pref)
    while t > mult and (n % t or t % mult):
        t -= mult
    assert n % t == 0, (n, pref)
    return t


def _params(sem):
    return pltpu.CompilerParams(dimension_semantics=sem, vmem_limit_bytes=VMEM_LIMIT)


def _silu(v):
    return v * jax.nn.sigmoid(v)


def _dot(a, b):
    return jnp.dot(a.astype(BF), b.astype(BF), preferred_element_type=F32)


def _mod_kernel(c_ref, w_ref, b_ref, o_ref):
    c = c_ref[...]
    o_ref[...] = jnp.dot(_silu(c), w_ref[...], precision=HIGHEST, preferred_element_type=F32) + b_ref[...]


def _modulation(c_all, w_mod, b_mod):
    r = c_all.shape[0]
    tn = 512
    return pl.pallas_call(
        _mod_kernel,
        grid=(3 * D // tn,),
        in_specs=[pl.BlockSpec((r, D), lambda n: (0, 0)),
                  pl.BlockSpec((D, tn), lambda n: (0, n)),
                  pl.BlockSpec((1, tn), lambda n: (0, n))],
        out_specs=pl.BlockSpec((r, tn), lambda n: (0, n)),
        out_shape=jax.ShapeDtypeStruct((r, 3 * D), F32),
        compiler_params=_params(("arbitrary",)),
        name="modulation",
    )(c_all, w_mod, b_mod.reshape(1, 3 * D))


def _inproj_kernel(x_ref, sc_ref, sh_ref, g_ref, *refs, n_w):
    w_refs, o_refs = refs[:n_w], refs[n_w:]
    x = x_ref[0]
    y = x * lax.rsqrt(jnp.mean(x * x, axis=-1, keepdims=True) + EPS) * g_ref[...]
    hb = (y * (1.0 + sc_ref[0]) + sh_ref[0]).astype(BF)
    for w_ref, o_ref in zip(w_refs, o_refs):
        n = w_ref.shape[1]
        step = 512 if n % 512 == 0 else n
        for c0 in range(0, n, step):
            o_ref[0, :, c0:c0 + step] = jnp.dot(
                hb, w_ref[:, c0:c0 + step], preferred_element_type=F32).astype(o_ref.dtype)


def _inproj(x, scale, shift, g_pre, weights, out_dtypes, per_batch_mod):
    b, l, _ = x.shape
    tm = _tile(l, 512)
    mod_map = (lambda bi, i: (bi, 0, 0)) if per_batch_mod else (lambda bi, i: (0, 0, 0))
    in_specs = [pl.BlockSpec((1, tm, D), lambda bi, i: (bi, i, 0)),
                pl.BlockSpec((1, 1, D), mod_map),
                pl.BlockSpec((1, 1, D), mod_map),
                pl.BlockSpec((1, D), lambda bi, i: (0, 0))]
    in_specs += [pl.BlockSpec(w.shape, lambda bi, i: (0, 0)) for w in weights]
    out_specs = [pl.BlockSpec((1, tm, w.shape[1]), lambda bi, i: (bi, i, 0)) for w in weights]
    out_shape = [jax.ShapeDtypeStruct((b, l, w.shape[1]), dt) for w, dt in zip(weights, out_dtypes)]
    return pl.pallas_call(
        functools.partial(_inproj_kernel, n_w=len(weights)),
        grid=(b, l // tm),
        in_specs=in_specs, out_specs=out_specs, out_shape=out_shape,
        compiler_params=_params(("parallel", "parallel")),
        name="inproj",
    )(x, scale, shift, g_pre.reshape(1, D), *weights)


def _dft_kernel(c_ref, s_ref, x_ref, o_ref, accp, accq):
    k = pl.program_id(2)

    @pl.when(k == 0)
    def _():
        accp[...] = jnp.zeros_like(accp)
        accq[...] = jnp.zeros_like(accq)

    xk = x_ref[0]
    accp[...] += jnp.dot(c_ref[...], xk, preferred_element_type=F32)
    accq[...] += jnp.dot(s_ref[...], xk, preferred_element_type=F32)

    @pl.when(k == pl.num_programs(2) - 1)
    def _():
        o_ref[0, :, :F_W] = accp[...].astype(o_ref.dtype)
        o_ref[0, :, F_W:] = accq[...].astype(o_ref.dtype)


def _dft_tables(l):
    blk = min(l, 64)
    j1 = jnp.arange(l // blk, dtype=jnp.int32)[:, None] * blk
    j2 = jnp.arange(blk, dtype=jnp.int32)[:, None]
    k = jnp.arange(l, dtype=jnp.int32)[None, :]
    w = 2.0 * math.pi / l
    a = ((j1 * k) % l).astype(F32) * w
    bb = ((j2 * k) % l).astype(F32) * w
    ca, sa, cb, sb = jnp.cos(a), jnp.sin(a), jnp.cos(bb), jnp.sin(bb)
    nrm = l ** -0.5
    cos = (ca[:, None, :] * cb[None] - sa[:, None, :] * sb[None]).reshape(l, l) * nrm
    sin = (sa[:, None, :] * cb[None] + ca[:, None, :] * sb[None]).reshape(l, l) * nrm
    return cos.astype(BF), sin.astype(BF)


def _dft(fa_x, cos_t, sin_t):
    b, l, _ = fa_x.shape
    tm = _tile(l, 1024)
    tk = _tile(l, 1024)
    return pl.pallas_call(
        _dft_kernel,
        grid=(b, l // tm, l // tk),
        in_specs=[pl.BlockSpec((tm, tk), lambda bi, i, k: (i, k)),
                  pl.BlockSpec((tm, tk), lambda bi, i, k: (i, k)),
                  pl.BlockSpec((1, tk, F_W), lambda bi, i, k: (bi, k, 0))],
        out_specs=pl.BlockSpec((1, tm, 2 * F_W), lambda bi, i, k: (bi, i, 0)),
        out_shape=jax.ShapeDtypeStruct((b, l, 2 * F_W), BF),
        scratch_shapes=[pltpu.VMEM((tm, F_W), F32), pltpu.VMEM((tm, F_W), F32)],
        compiler_params=_params(("parallel", "parallel", "arbitrary")),
        name="dft",
    )(cos_t, sin_t, fa_x)


def _gdn_prep_kernel(x_ref, prev_ref, next_ref, ab_ref, cw_ref, alog_ref, dtb_ref,
                     q_ref, k_ref, v_ref, g_ref):
    i = pl.program_id(1)
    last = pl.num_programs(1) - 1
    x = x_ref[0].astype(F32)
    tm = x.shape[0]
    hr = prev_ref.shape[1]
    prev_row = prev_ref[0, hr - 1:hr, :].astype(F32) * (i > 0).astype(F32)
    next_row = next_ref[0, 0:1, :].astype(F32) * (i < last).astype(F32)
    rows = lax.broadcasted_iota(jnp.int32, x.shape, 0)
    x_dn = jnp.where(rows == 0, prev_row, pltpu.roll(x, 1, axis=0))
    x_up = jnp.where(rows == tm - 1, next_row, pltpu.roll(x, tm - 1, axis=0))
    cw = cw_ref[...]
    y = _silu(cw[0:1, :] * x_dn + cw[1:2, :] * x + cw[2:3, :] * x_up)
    for h in range(DN_H):
        qs = y[:, h * DN_DK:(h + 1) * DN_DK]
        ks = y[:, DN_W + h * DN_DK:DN_W + (h + 1) * DN_DK]
        qn = qs * lax.rsqrt(jnp.sum(qs * qs, axis=-1, keepdims=True) + EPS) * (DN_DK ** -0.5)
        kn = ks * lax.rsqrt(jnp.sum(ks * ks, axis=-1, keepdims=True) + EPS)
        q_ref[0, :, h * DN_DK:(h + 1) * DN_DK] = qn.astype(q_ref.dtype)
        k_ref[0, :, h * DN_DK:(h + 1) * DN_DK] = kn.astype(k_ref.dtype)
    v_ref[0] = y[:, 2 * DN_W:].astype(v_ref.dtype)
    a = ab_ref[0]
    cols = lax.broadcasted_iota(jnp.int32, a.shape, 1)
    z = a + dtb_ref[...]
    softplus = jnp.maximum(z, 0.0) + jnp.log1p(jnp.exp(-jnp.abs(z)))
    g = -jnp.exp(alog_ref[...]) * softplus
    g_ref[0] = jnp.where(cols < 2 * DN_H, g, jnp.where(cols < 4 * DN_H, jax.nn.sigmoid(a), 0.0))


def _gdn_prep(qkv, ab, conv_w, a_log, dt_bias):
    b, l, w = qkv.shape
    tm = _tile(l, 512)
    hr = 16
    nb = tm // hr
    pad = jnp.zeros((1, LANE - 2 * DN_H), F32)
    alog = jnp.concatenate([a_log.reshape(1, 2 * DN_H).astype(F32), pad], axis=1)
    dtb = jnp.concatenate([dt_bias.reshape(1, 2 * DN_H).astype(F32), pad], axis=1)
    last_blk = l // hr - 1
    outs = pl.pallas_call(
        _gdn_prep_kernel,
        grid=(b, l // tm),
        in_specs=[pl.BlockSpec((1, tm, w), lambda bi, i: (bi, i, 0)),
                  pl.BlockSpec((1, hr, w), lambda bi, i: (bi, jnp.maximum(i * nb - 1, 0), 0)),
                  pl.BlockSpec((1, hr, w), lambda bi, i: (bi, jnp.minimum((i + 1) * nb, last_blk), 0)),
                  pl.BlockSpec((1, tm, LANE), lambda bi, i: (bi, i, 0)),
                  pl.BlockSpec((3, w), lambda bi, i: (0, 0)),
                  pl.BlockSpec((1, LANE), lambda bi, i: (0, 0)),
                  pl.BlockSpec((1, LANE), lambda bi, i: (0, 0))],
        out_specs=[pl.BlockSpec((1, tm, DN_W), lambda bi, i: (bi, i, 0)),
                   pl.BlockSpec((1, tm, DN_W), lambda bi, i: (bi, i, 0)),
                   pl.BlockSpec((1, tm, DN_W), lambda bi, i: (bi, i, 0)),
                   pl.BlockSpec((1, tm, LANE), lambda bi, i: (bi, i, 0))],
        out_shape=[jax.ShapeDtypeStruct((b, l, DN_W), BF)] * 3 + [jax.ShapeDtypeStruct((b, l, LANE), F32)],
        compiler_params=_params(("parallel", "parallel")),
        name="gdn_prep",
    )(qkv, qkv, qkv, ab, conv_w.astype(F32), alog, dtb)
    return outs


def _nt_dot(a, b):
    return lax.dot_general(a, b, (((1,), (1,)), ((), ())), preferred_element_type=F32)


def _gdn_local_kernel(q_ref, k_ref, v_ref, g_ref, a1_ref, a2_ref, u_ref, egl_ref, *, cg):
    c = DN_CHUNK
    ri = lax.broadcasted_iota(jnp.int32, (c, c), 0)
    ci = lax.broadcasted_iota(jnp.int32, (c, c), 1)
    eye = (ri == ci).astype(F32)
    lvl_masks = []
    bsz = 1
    while bsz < c:
        lvl_masks.append((ri // (2 * bsz) == ci // (2 * bsz)) & (ri // bsz != ci // bsz))
        bsz *= 2
    incl = (ri >= ci, ri <= ci)
    strict = (ri > ci, ri < ci)
    last = (c - 1, 0)

    systems = []
    for cc in range(cg):
        rows = slice(cc * c, (cc + 1) * c)
        ga = g_ref[0, rows, :]
        heads = []
        for h in range(DN_H):
            cols = slice(h * DN_DK, (h + 1) * DN_DK)
            q, k, v = q_ref[0, rows, cols], k_ref[0, rows, cols], v_ref[0, rows, cols]
            heads.append((q, k, v, _nt_dot(k, k), _nt_dot(q, k)))
        for d in range(2):
            gcum = jnp.dot(incl[d].astype(F32), ga, precision=HIGHEST, preferred_element_type=F32)
            gcum_t = jnp.concatenate([gcum, jnp.zeros_like(gcum)], axis=0).T
            for h in range(DN_H):
                q, k, v, kk, qk = heads[h]
                col = d * DN_H + h
                gc = gcum[:, col:col + 1]
                gr = gcum_t[col:col + 1, :c]
                bc = ga[:, 2 * DN_H + col:2 * DN_H + col + 1]
                gl = gcum[last[d]:last[d] + 1, col:col + 1]
                decay = jnp.where(incl[d], jnp.exp(jnp.where(incl[d], gc - gr, 0.0)), 0.0)
                a_m = jnp.where(strict[d], bc * kk * decay, 0.0)
                systems.append(dict(idx=(d, cc, h), q=q.astype(F32), k=k.astype(F32), v=v.astype(F32), gc=gc, bc=bc,
                                    gl=gl, eg=jnp.exp(gc), qk=qk * decay, a_m=a_m,
                                    t=eye - jnp.where(lvl_masks[0], a_m, 0.0)))
    for lm in lvl_masks[1:]:
        inner = [_dot(jnp.where(lm, s["a_m"], 0.0), s["t"]) for s in systems]
        for s, m in zip(systems, inner):
            s["t"] = s["t"] - _dot(s["t"], m)
    sols = [_dot(s["t"], jnp.concatenate([(s["bc"] * s["eg"]) * s["k"], s["bc"] * s["v"]], axis=1))
            for s in systems]
    for s, sol in zip(systems, sols):
        d, cc, h = s["idx"]
        a1_ref[0, d, cc, h, :c, :] = sol[:, :DN_DK].astype(BF)
        a1_ref[0, d, cc, h, c:, :] = (s["q"] * s["eg"]).astype(BF)
        a2_ref[0, d, cc, h, :c, :] = s["qk"].astype(BF)
        a2_ref[0, d, cc, h, c:, :] = (s["k"] * jnp.exp(s["gl"] - s["gc"])).T.astype(BF)
        u_ref[0, d, cc, h] = sol[:, DN_DK:].astype(BF)
        egl_ref[0, d, cc, h:h + 1, :] = jnp.broadcast_to(jnp.exp(s["gl"]), (1, LANE))


def _gdn_local(qn, kn, v, gact):
    b, l, _ = qn.shape
    n = l // DN_CHUNK
    cg = 2 if n % 2 == 0 else 1
    c = DN_CHUNK
    row = lambda w: pl.BlockSpec((1, cg * c, w), lambda bi, i: (bi, i, 0))
    out = lambda r, w: pl.BlockSpec((1, 2, cg, DN_H, r, w), lambda bi, i: (bi, 0, i, 0, 0, 0))
    return pl.pallas_call(
        functools.partial(_gdn_local_kernel, cg=cg),
        grid=(b, n // cg),
        in_specs=[row(DN_W), row(DN_W), row(DN_W), row(LANE)],
        out_specs=[out(2 * c, DN_DK), out(c + DN_DK, c), out(c, DN_DK),
                   pl.BlockSpec((1, 2, cg, DN_H, LANE), lambda bi, i: (bi, 0, i, 0, 0))],
        out_shape=[jax.ShapeDtypeStruct((b, 2, n, DN_H, 2 * c, DN_DK), BF),
                   jax.ShapeDtypeStruct((b, 2, n, DN_H, c + DN_DK, c), BF),
                   jax.ShapeDtypeStruct((b, 2, n, DN_H, c, DN_DK), BF),
                   jax.ShapeDtypeStruct((b, 2, n, DN_H, LANE), F32)],
        compiler_params=_params(("parallel", "parallel")),
        name="gdn_local",
    )(qn, kn, v, gact)


def _gdn_scan_kernel(a1f, a2f, uf, ef, a1b, a2b, ub, eb, s0_ref, of_ref, ob_ref, s_ref, *, bg):
    j = pl.program_id(1)

    @pl.when(j == 0)
    def _():
        s_ref[...] = s0_ref[...]

    c = DN_CHUNK
    dirs = ((a1f, a2f, uf, ef, of_ref), (a1b, a2b, ub, eb, ob_ref))
    chains = [(bb, d, h) for bb in range(bg) for d in range(2) for h in range(DN_H)]
    st = [s_ref[bb, d, h] for bb, d, h in chains]
    r1 = [jnp.dot(dirs[d][0][bb, 0, 0, h], s.astype(BF), preferred_element_type=F32)
          for (bb, d, h), s in zip(chains, st)]
    un = [dirs[d][2][bb, 0, 0, h].astype(F32) - r[:c] for (bb, d, h), r in zip(chains, r1)]
    r2 = [jnp.dot(dirs[d][1][bb, 0, 0, h], u.astype(BF), preferred_element_type=F32)
          for (bb, d, h), u in zip(chains, un)]
    for (bb, d, h), s, ra, rb in zip(chains, st, r1, r2):
        dirs[d][4][bb, :, h * DN_DK:(h + 1) * DN_DK] = (ra[c:] + rb[:c]).astype(of_ref.dtype)
        s_ref[bb, d, h] = s * dirs[d][3][bb, 0, 0, h:h + 1, :] + rb[c:]


def _gdn_scan(a1, a2, u, egl, s0):
    b, _, n, _, _, _ = a1.shape
    c = DN_CHUNK
    bg = max(g for g in (1, 2, 4) if b % g == 0)
    fwd = lambda bi, j: (bi, 0, j, 0, 0, 0)
    bwd = lambda bi, j: (bi, 1, n - 1 - j, 0, 0, 0)
    blk = lambda a, m: pl.BlockSpec((bg, 1, 1) + a.shape[3:], m)
    eblk = lambda m: pl.BlockSpec((bg, 1, 1, DN_H, LANE), lambda bi, j: m(bi, j)[:5])
    st = pl.BlockSpec((bg, 2, DN_H, DN_DK, DN_DK), lambda bi, j: (bi, 0, 0, 0, 0))
    return pl.pallas_call(
        functools.partial(_gdn_scan_kernel, bg=bg),
        grid=(b // bg, n),
        in_specs=[blk(a1, fwd), blk(a2, fwd), blk(u, fwd), eblk(fwd),
                  blk(a1, bwd), blk(a2, bwd), blk(u, bwd), eblk(bwd), st],
        out_specs=[pl.BlockSpec((bg, c, DN_W), lambda bi, j: (bi, j, 0)),
                   pl.BlockSpec((bg, c, DN_W), lambda bi, j: (bi, n - 1 - j, 0)), st],
        out_shape=[jax.ShapeDtypeStruct((b, n * c, DN_W), BF), jax.ShapeDtypeStruct((b, n * c, DN_W), BF),
                   jax.ShapeDtypeStruct((b, 2, DN_H, DN_DK, DN_DK), F32)],
        compiler_params=_params(("parallel", "arbitrary")),
        name="gdn_scan",
    )(a1, a2, u, egl, a1, a2, u, egl, s0)


def _mla_prep_kernel(m_ref, qn_ref, kvn_ref, wq_ref, wqr_ref, wk_ref, wv_ref, cos_ref, sin_ref,
                     q_ref, k_ref, v_ref):
    m = m_ref[0].astype(F32)
    cq = m[:, :M_QL]
    ckv = m[:, M_QL:M_QL + M_KVL]
    kpa = m[:, M_QL + M_KVL:M_QL + M_KVL + LANE]
    kpb = m[:, M_QL + M_KVL + LANE:]
    hq = (cq * lax.rsqrt(jnp.mean(cq * cq, axis=-1, keepdims=True) + EPS) * qn_ref[...]).astype(BF)
    hkv = (ckv * lax.rsqrt(jnp.mean(ckv * ckv, axis=-1, keepdims=True) + EPS) * kvn_ref[...]).astype(BF)
    cos = cos_ref[...]
    sin = sin_ref[...]
    scale = (M_NOPE + M_ROPE) ** -0.5
    q_all = jnp.dot(hq, wq_ref[...], preferred_element_type=F32)
    q_rot = jnp.dot(hq, wqr_ref[...], preferred_element_type=F32)
    k_all = jnp.dot(hkv, wk_ref[...], preferred_element_type=F32)
    v_all = jnp.dot(hkv, wv_ref[...], preferred_element_type=F32)
    kpe = kpa * cos + kpb * sin
    lanes = lax.broadcasted_iota(jnp.int32, kpe.shape, 1)
    one_col = jnp.where(lanes == M_V, 1.0, 0.0)
    for h in range(M_H):
        sl = slice(h * M_HP, (h + 1) * M_HP)
        q_ref[0, h] = ((q_all[:, sl] * cos + q_rot[:, sl] * sin) * scale).astype(q_ref.dtype)
        k_ref[0, h] = (k_all[:, sl] + kpe).astype(k_ref.dtype)
        v_ref[0, h] = (v_all[:, sl] + one_col).astype(v_ref.dtype)


def _mla_prep(mla_in, q_norm, kv_norm, wq, wqr, wk, wv, cos_t, sin_t):
    b, l, w = mla_in.shape
    tm = _tile(l, 512)
    full = lambda a: pl.BlockSpec(a.shape, lambda bi, i: (0,) * a.ndim)
    hw = M_H * M_HP
    return pl.pallas_call(
        _mla_prep_kernel,
        grid=(b, l // tm),
        in_specs=[pl.BlockSpec((1, tm, w), lambda bi, i: (bi, i, 0)),
                  pl.BlockSpec((1, M_QL), lambda bi, i: (0, 0)),
                  pl.BlockSpec((1, M_KVL), lambda bi, i: (0, 0)),
                  full(wq), full(wqr), full(wk), full(wv),
                  pl.BlockSpec((tm, LANE), lambda bi, i: (i, 0)),
                  pl.BlockSpec((tm, LANE), lambda bi, i: (i, 0))],
        out_specs=[pl.BlockSpec((1, M_H, tm, M_HP), lambda bi, i: (bi, 0, i, 0))] * 3,
        out_shape=[jax.ShapeDtypeStruct((b, M_H, l, M_HP), BF)] * 3,
        compiler_params=_params(("parallel", "parallel")),
        name="mla_prep",
    )(mla_in, q_norm.reshape(1, M_QL).astype(F32), kv_norm.reshape(1, M_KVL).astype(F32),
      wq, wqr, wk, wv, cos_t, sin_t)


def _attn_kernel(q_ref, *refs, n_src):
    k_refs, v_refs, o_ref = refs[:n_src], refs[n_src:2 * n_src], refs[2 * n_src]
    tq = q_ref.shape[2]
    sub = min(tq, ATT_SUB)
    units = [(hh, r0) for hh in range(2) for r0 in range(0, tq, sub)]

    def scores(u):
        hh, r0 = u
        return [_nt_dot(q_ref[0, hh, r0:r0 + sub, :], k_r[0, hh]) for k_r in k_refs]

    def finish(u, s):
        hh, r0 = u
        m = s[0].max(axis=-1, keepdims=True)
        for si in s[1:]:
            m = jnp.maximum(m, si.max(axis=-1, keepdims=True))
        acc = None
        for si, v_r in zip(s, v_refs):
            part = jnp.dot(jnp.exp(si - m).astype(BF), v_r[0, hh], preferred_element_type=F32)
            acc = part if acc is None else acc + part
        o_ref[0, r0:r0 + sub, hh * M_V:(hh + 1) * M_V] = (acc[:, :M_V] / acc[:, M_V:M_V + 1]).astype(o_ref.dtype)

    s_cur = scores(units[0])
    for i, u in enumerate(units):
        s_nxt = scores(units[i + 1]) if i + 1 < len(units) else None
        finish(u, s_cur)
        s_cur = s_nxt


def _attention(q, ks, vs):
    b, _, l, _ = q.shape
    tq = _tile(l, 2 * ATT_SUB)
    n_src = len(ks)
    kv_spec = lambda a: pl.BlockSpec((1, 2, a.shape[2], M_HP), lambda bi, hp, i: (bi, hp, 0, 0))
    return pl.pallas_call(
        functools.partial(_attn_kernel, n_src=n_src),
        grid=(b, M_H // 2, l // tq),
        in_specs=[pl.BlockSpec((1, 2, tq, M_HP), lambda bi, hp, i: (bi, hp, i, 0))]
                 + [kv_spec(a) for a in ks] + [kv_spec(a) for a in vs],
        out_specs=pl.BlockSpec((1, tq, 2 * M_V), lambda bi, hp, i: (bi, i, hp)),
        out_shape=jax.ShapeDtypeStruct((b, l, M_H * M_V), BF),
        compiler_params=_params(("parallel", "parallel", "arbitrary")),
        name="attention",
    )(q, *ks, *vs)


def _merge_kernel(pq_ref, faz_ref, of_ref, ob_ref, dnz_ref, om_ref, mz_ref, gl_ref, x_ref, gm_ref, gp_ref,
                  fw_ref, cc_ref, sc_ref, dnn_ref, wb_ref, wo_ref, o_ref):
    pq = pq_ref[0]
    faz = faz_ref[0].astype(F32)
    ya = []
    for g in range(F_GROUPS):
        sl = slice(g * F_GD, (g + 1) * F_GD)
        spec = (jnp.dot(pq[:, sl], cc_ref[...], preferred_element_type=F32)
                - jnp.dot(pq[:, F_W + g * F_GD:F_W + (g + 1) * F_GD], sc_ref[...], preferred_element_type=F32))
        t = jnp.dot(spec.astype(BF), fw_ref[g], preferred_element_type=F32)
        ya.append((t * _silu(faz[:, sl])).astype(BF))
    ya = jnp.concatenate(ya, axis=1)
    osum = of_ref[0].astype(F32) + ob_ref[0].astype(F32)
    dnz = dnz_ref[0].astype(F32)
    yb = []
    for h in range(DN_H):
        sl = slice(h * DN_DK, (h + 1) * DN_DK)
        seg = osum[:, sl]
        nrm = seg * lax.rsqrt(jnp.mean(seg * seg, axis=-1, keepdims=True) + EPS) * dnn_ref[...]
        yb.append((nrm * _silu(dnz[:, sl])).astype(BF))
    yb = jnp.concatenate(yb, axis=1)
    yc = (om_ref[0].astype(F32) * _silu(mz_ref[0].astype(F32))).astype(BF)
    merged = None
    for idx, yy in enumerate((ya, yb, yc)):
        gate = jax.nn.sigmoid(gl_ref[0, :, idx * D:(idx + 1) * D].astype(F32))
        term = gate * jnp.dot(yy, wb_ref[idx], preferred_element_type=F32)
        merged = term if merged is None else merged + term
    y = jnp.dot(merged.astype(BF), wo_ref[...], preferred_element_type=F32)
    yn = y * lax.rsqrt(jnp.mean(y * y, axis=-1, keepdims=True) + EPS) * gp_ref[...]
    o_ref[0] = x_ref[0] + gm_ref[0] * yn


def _merge(pq, fa_z, o_f, o_b, dn_z, o_m, mla_z, gate_logits, x, gate_mod, g_post, fw, cc, sc, dn_norm, wb, wo,
           per_batch_mod):
    b, l, _ = x.shape
    tm = _tile(l, 512)
    row = lambda w: pl.BlockSpec((1, tm, w), lambda bi, i: (bi, i, 0))
    full = lambda a: pl.BlockSpec(a.shape, lambda bi, i: (0,) * a.ndim)
    mod_map = (lambda bi, i: (bi, 0, 0)) if per_batch_mod else (lambda bi, i: (0, 0, 0))
    gp = g_post.reshape(1, D).astype(F32)
    dnn = dn_norm.reshape(1, DN_DK).astype(F32)
    return pl.pallas_call(
        _merge_kernel,
        grid=(b, l // tm),
        in_specs=[row(2 * F_W), row(F_W), row(DN_W), row(DN_W), row(DN_W), row(512), row(512), row(3 * D), row(D),
                  pl.BlockSpec((1, 1, D), mod_map), full(gp), full(fw), full(cc), full(sc), full(dnn),
                  full(wb), full(wo)],
        out_specs=row(D),
        out_shape=jax.ShapeDtypeStruct((b, l, D), F32),
        compiler_params=_params(("parallel", "parallel")),
        name="merge",
    )(pq, fa_z, o_f, o_b, dn_z, o_m, mla_z, gate_logits, x, gate_mod, gp, fw, cc, sc, dnn, wb, wo)


def _rot_cols(w):
    q = M_ROPE // 4
    return jnp.concatenate([-w[..., q:2 * q], w[..., :q], -w[..., 3 * q:], w[..., 2 * q:3 * q]], axis=-1)


def _layer_weights(w_in, mla_w_uq, mla_w_ukv):
    col = lambda n: w_in[:, _OFF[n][0]:_OFF[n][1]]
    zeros = lambda n: jnp.zeros((D, n), F32)
    w_qkv = jnp.concatenate([col("dn_q"), col("dn_k"), col("dn_v")], axis=1)
    w_ab = jnp.concatenate([col("dn_ab"), zeros(LANE - 4 * DN_H)], axis=1)
    kpe = col("kpe")
    rest = LANE - M_NOPE - M_ROPE
    kpe_a = jnp.concatenate([zeros(M_NOPE), kpe, zeros(rest)], axis=1)
    kpe_b = jnp.concatenate([zeros(M_NOPE), _rot_cols(kpe), zeros(rest)], axis=1)
    w_mla = jnp.concatenate([col("cq"), col("ckv"), kpe_a, kpe_b], axis=1)
    grp1 = [col("fa_x"), col("fa_z"), w_qkv, col("dn_z"), w_ab]
    grp2 = [w_mla, col("mla_z"), col("gate")]
    uq = mla_w_uq.reshape(M_QL, M_H, M_NOPE + M_ROPE)
    zq = jnp.zeros((M_QL, M_H, rest), F32)
    wq = jnp.concatenate([uq, zq], axis=-1).reshape(M_QL, M_H * M_HP)
    wqr = jnp.concatenate([jnp.zeros((M_QL, M_H, M_NOPE), F32), _rot_cols(uq[..., M_NOPE:]), zq],
                          axis=-1).reshape(M_QL, M_H * M_HP)
    ukv = mla_w_ukv.reshape(M_KVL, M_H, M_NOPE + M_V)
    zk = jnp.zeros((M_KVL, M_H, M_HP - M_NOPE), F32)
    wk = jnp.concatenate([ukv[..., :M_NOPE], zk], axis=-1).reshape(M_KVL, M_H * M_HP)
    wv = jnp.concatenate([ukv[..., M_NOPE:], jnp.zeros((M_KVL, M_H, M_HP - M_V), F32)],
                         axis=-1).reshape(M_KVL, M_H * M_HP)
    cast = lambda ws: [w.astype(BF) for w in ws]
    return cast(grp1), cast(grp2), cast([wq, wqr, wk, wv])


def _rope_tables(l, rotary):
    ones = jnp.ones((l, M_NOPE), F32)
    pad = jnp.zeros((l, LANE - M_NOPE - M_ROPE), F32)
    if not rotary:
        return (jnp.concatenate([ones, jnp.ones((l, M_ROPE), F32), pad], axis=1),
                jnp.zeros((l, LANE), F32))
    pos = jnp.arange(l, dtype=jnp.int32)
    row = (pos // GRID_W).astype(F32)
    colp = (pos % GRID_W).astype(F32)
    n_freq = M_ROPE // 4
    inv = ROPE_BASE ** (-jnp.arange(n_freq, dtype=F32) / n_freq)
    ang_r = row[:, None] * inv
    ang_c = colp[:, None] * inv
    ang = jnp.concatenate([ang_r, ang_r, ang_c, ang_c], axis=-1)
    return (jnp.concatenate([ones, jnp.cos(ang), pad], axis=1),
            jnp.concatenate([jnp.zeros((l, M_NOPE), F32), jnp.sin(ang), pad], axis=1))


def _chan_tables():
    j = jnp.arange(F_GD, dtype=jnp.int32)
    a = ((j[:, None] * j[None, :]) % F_GD).astype(F32) * (2.0 * math.pi / F_GD)
    nrm = F_GD ** -0.5
    return (jnp.cos(a) * nrm).astype(BF), (jnp.sin(a) * nrm).astype(BF)


def _layer(x, ctx, mod, p, tables, need_ctx_out):
    b = x.shape[0]
    shift, scale, gate = (mod[:b, None, i * D:(i + 1) * D] for i in range(3))
    shift_c, scale_c, gate_c = (mod[b:b + 1, None, i * D:(i + 1) * D] for i in range(3))
    grp1, grp2, (wq, wqr, wk, wv) = _layer_weights(p["w_in"], p["mla_w_uq"], p["mla_w_ukv"])
    dts1 = [BF, BF, BF, BF, F32]
    dts2 = [BF, BF, BF]

    def project(t, sc, sh, per_batch):
        a = _inproj(t, sc, sh, p["g_pre"], grp1, dts1, per_batch)
        bb = _inproj(t, sc, sh, p["g_pre"], grp2, dts2, per_batch)
        return a, bb

    (fa_x, fa_z, qkv, dn_z, ab), (mla_in, mla_z, gl) = project(x, scale, shift, True)
    (fa_xc, fa_zc, qkvc, dn_zc, abc), (mla_inc, mla_zc, glc) = project(ctx, scale_c, shift_c, False)

    qn, kn, vv, gact = _gdn_prep(qkv, ab, p["dn_conv"], p["dn_a_log"], p["dn_dt_bias"])
    qnc, knc, vvc, gactc = _gdn_prep(qkvc, abc, p["dn_conv"], p["dn_a_log"], p["dn_dt_bias"])
    s0 = jnp.zeros((b, 2, DN_H, DN_DK, DN_DK), F32)
    ofc, obc, s_ctx = _gdn_scan(*_gdn_local(qnc, knc, vvc, gactc), s0)
    of, ob, _ = _gdn_scan(*_gdn_local(qn, kn, vv, gact), s_ctx)

    q_m, k_m, v_m = _mla_prep(mla_in, p["mla_q_norm"], p["mla_kv_norm"], wq, wqr, wk, wv, *tables["rope"])
    qc_m, kc_m, vc_m = _mla_prep(mla_inc, p["mla_q_norm"], p["mla_kv_norm"], wq, wqr, wk, wv, *tables["rope_c"])
    o_m = _attention(q_m, [k_m, kc_m], [v_m, vc_m])

    pq = _dft(fa_x, *tables["dft"])
    fw = p["f_w"].astype(BF)
    wb = p["w_branch"].astype(BF)
    wo = p["w_out"].astype(BF)
    cc, sc = tables["chan"]
    x_new = _merge(pq, fa_z, of, ob, dn_z, o_m, mla_z, gl, x, gate, p["g_post"], fw, cc, sc, p["dn_norm"],
                   wb, wo, True)
    ctx_new = ctx
    if need_ctx_out:
        pqc = _dft(fa_xc, *tables["dft_c"])
        oc_m = _attention(qc_m, [kc_m], [vc_m])
        ctx_new = _merge(pqc, fa_zc, ofc, obc, dn_zc, oc_m, mla_zc, glc, ctx, gate_c, p["g_post"], fw, cc, sc,
                         p["dn_norm"], wb, wo, False)
    return x_new, ctx_new


def kernel(x, c, ctx, c_ctx, w_mod, b_mod, g_pre, g_post, w_in, f_w, dn_conv, dn_a_log, dn_dt_bias, dn_norm,
           mla_q_norm, mla_w_uq, mla_kv_norm, mla_w_ukv, w_branch, w_out):
    b, l, _ = x.shape
    lc = ctx.shape[1]
    depth = w_mod.shape[0]
    rows = -(-(b + 1) // 8) * 8
    c_all = jnp.concatenate([c, c_ctx[None, :], jnp.zeros((rows - b - 1, D), F32)], axis=0)
    tables = {"dft": _dft_tables(l), "dft_c": _dft_tables(lc), "chan": _chan_tables(),
              "rope": _rope_tables(l, True), "rope_c": _rope_tables(lc, False)}
    per_layer = dict(g_pre=g_pre, g_post=g_post, w_in=w_in, f_w=f_w, dn_conv=dn_conv, dn_a_log=dn_a_log,
                     dn_dt_bias=dn_dt_bias, dn_norm=dn_norm, mla_q_norm=mla_q_norm, mla_w_uq=mla_w_uq,
                     mla_kv_norm=mla_kv_norm, mla_w_ukv=mla_w_ukv, w_branch=w_branch, w_out=w_out)
    for li in range(depth):
        p = {k: v[li] for k, v in per_layer.items()}
        mod = _modulation(c_all, w_mod[li], b_mod[li])
        x, ctx = _layer(x, ctx, mod, p, tables, need_ctx_out=(li < depth - 1))
    return x
```

```python
import functools
import math

import jax
import jax.numpy as jnp
from jax import lax
from jax.experimental import pallas as pl
from jax.experimental.pallas import tpu as pltpu

F32 = jnp.float32
BF = jnp.bfloat16
HIGHEST = lax.Precision.HIGHEST

D = 1024
EPS = 1e-6
GRID_W = 64
F_GROUPS = 4
F_GD = 128
F_W = 512
DN_H = 4
DN_DK = 128
DN_W = 512
DN_CHUNK = 64
M_H = 8
M_QL = 384
M_KVL = 256
M_NOPE = 64
M_ROPE = 32
M_V = 64
M_HP = 128
ROPE_BASE = 10000.0
LANE = 128
VMEM_LIMIT = 56 * 1024 * 1024
ATT_SUB = 512
ATT_KSPLIT = 1
ATT_VROWS = 80

_OFF = {}
_o = 0
for _n, _w in (("fa_x", 512), ("fa_z", 512), ("dn_q", 512), ("dn_k", 512), ("dn_v", 512), ("dn_z", 512),
               ("dn_ab", 16), ("cq", M_QL), ("ckv", M_KVL), ("kpe", M_ROPE), ("mla_z", 512), ("gate", 3 * D)):
    _OFF[_n] = (_o, _o + _w)
    _o += _w


def _tile(n, pref, mult=16):
    t = min(n, pref)
    while t > mult and (n % t or t % mult):
        t -= mult
    assert n % t == 0, (n, pref)
    return t


def _params(sem):
    return pltpu.CompilerParams(dimension_semantics=sem, vmem_limit_bytes=VMEM_LIMIT)


def _silu(v):
    return v * jax.nn.sigmoid(v)


def _dot(a, b):
    return jnp.dot(a.astype(BF), b.astype(BF), preferred_element_type=F32)


def _mod_kernel(c_ref, w_ref, b_ref, o_ref):
    c = c_ref[...]
    o_ref[...] = jnp.dot(_silu(c), w_ref[...], precision=HIGHEST, preferred_element_type=F32) + b_ref[...]


def _modulation(c_all, w_mod, b_mod):
    r = c_all.shape[0]
    tn = 512
    return pl.pallas_call(
        _mod_kernel,
        grid=(3 * D // tn,),
        in_specs=[pl.BlockSpec((r, D), lambda n: (0, 0)),
                  pl.BlockSpec((D, tn), lambda n: (0, n)),
                  pl.BlockSpec((1, tn), lambda n: (0, n))],
        out_specs=pl.BlockSpec((r, tn), lambda n: (0, n)),
        out_shape=jax.ShapeDtypeStruct((r, 3 * D), F32),
        compiler_params=_params(("arbitrary",)),
        name="modulation",
    )(c_all, w_mod, b_mod.reshape(1, 3 * D))


def _inproj_kernel(x_ref, sc_ref, sh_ref, g_ref, *refs, n_w):
    w_refs, o_refs = refs[:n_w], refs[n_w:]
    x = x_ref[0]
    y = x * lax.rsqrt(jnp.mean(x * x, axis=-1, keepdims=True) + EPS) * g_ref[...]
    hb = (y * (1.0 + sc_ref[0]) + sh_ref[0]).astype(BF)
    for w_ref, o_ref in zip(w_refs, o_refs):
        n = w_ref.shape[1]
        step = 512 if n % 512 == 0 else n
        for c0 in range(0, n, step):
            o_ref[0, :, c0:c0 + step] = jnp.dot(
                hb, w_ref[:, c0:c0 + step], preferred_element_type=F32).astype(o_ref.dtype)


def _inproj(x, scale, shift, g_pre, weights, out_dtypes, per_batch_mod):
    b, l, _ = x.shape
    tm = _tile(l, 512)
    mod_map = (lambda bi, i: (bi, 0, 0)) if per_batch_mod else (lambda bi, i: (0, 0, 0))
    in_specs = [pl.BlockSpec((1, tm, D), lambda bi, i: (bi, i, 0)),
                pl.BlockSpec((1, 1, D), mod_map),
                pl.BlockSpec((1, 1, D), mod_map),
                pl.BlockSpec((1, D), lambda bi, i: (0, 0))]
    in_specs += [pl.BlockSpec(w.shape, lambda bi, i: (0, 0)) for w in weights]
    out_specs = [pl.BlockSpec((1, tm, w.shape[1]), lambda bi, i: (bi, i, 0)) for w in weights]
    out_shape = [jax.ShapeDtypeStruct((b, l, w.shape[1]), dt) for w, dt in zip(weights, out_dtypes)]
    return pl.pallas_call(
        functools.partial(_inproj_kernel, n_w=len(weights)),
        grid=(b, l // tm),
        in_specs=in_specs, out_specs=out_specs, out_shape=out_shape,
        compiler_params=_params(("parallel", "parallel")),
        name="inproj",
    )(x, scale, shift, g_pre.reshape(1, D), *weights)


def _dft_kernel(c_ref, s_ref, x_ref, o_ref, accp, accq):
    k = pl.program_id(2)

    @pl.when(k == 0)
    def _():
        accp[...] = jnp.zeros_like(accp)
        accq[...] = jnp.zeros_like(accq)

    xk = x_ref[0]
    accp[...] += jnp.dot(c_ref[...], xk, preferred_element_type=F32)
    accq[...] += jnp.dot(s_ref[...], xk, preferred_element_type=F32)

    @pl.when(k == pl.num_programs(2) - 1)
    def _():
        o_ref[0, :, :F_W] = accp[...].astype(o_ref.dtype)
        o_ref[0, :, F_W:] = accq[...].astype(o_ref.dtype)


def _dft_tables(l):
    blk = min(l, 64)
    j1 = jnp.arange(l // blk, dtype=jnp.int32)[:, None] * blk
    j2 = jnp.arange(blk, dtype=jnp.int32)[:, None]
    k = jnp.arange(l, dtype=jnp.int32)[None, :]
    w = 2.0 * math.pi / l
    a = ((j1 * k) % l).astype(F32) * w
    bb = ((j2 * k) % l).astype(F32) * w
    ca, sa, cb, sb = jnp.cos(a), jnp.sin(a), jnp.cos(bb), jnp.sin(bb)
    nrm = l ** -0.5
    cos = (ca[:, None, :] * cb[None] - sa[:, None, :] * sb[None]).reshape(l, l) * nrm
    sin = (sa[:, None, :] * cb[None] + ca[:, None, :] * sb[None]).reshape(l, l) * nrm
    return cos.astype(BF), sin.astype(BF)


def _dft(fa_x, cos_t, sin_t):
    b, l, _ = fa_x.shape
    tm = _tile(l, 1024)
    tk = _tile(l, 1024)
    return pl.pallas_call(
        _dft_kernel,
        grid=(b, l // tm, l // tk),
        in_specs=[pl.BlockSpec((tm, tk), lambda bi, i, k: (i, k)),
                  pl.BlockSpec((tm, tk), lambda bi, i, k: (i, k)),
                  pl.BlockSpec((1, tk, F_W), lambda bi, i, k: (bi, k, 0))],
        out_specs=pl.BlockSpec((1, tm, 2 * F_W), lambda bi, i, k: (bi, i, 0)),
        out_shape=jax.ShapeDtypeStruct((b, l, 2 * F_W), BF),
        scratch_shapes=[pltpu.VMEM((tm, F_W), F32), pltpu.VMEM((tm, F_W), F32)],
        compiler_params=_params(("parallel", "parallel", "arbitrary")),
        name="dft",
    )(cos_t, sin_t, fa_x)


FFT_N2 = 64


def _fft_a_kernel(m_ref, x_ref, o_ref):
    n1 = x_ref.shape[1]
    r = jnp.dot(m_ref[...], x_ref[0], preferred_element_type=F32)
    o_ref[0, 0] = r[:n1].astype(o_ref.dtype)
    o_ref[0, 1] = r[n1:].astype(o_ref.dtype)


def _fft_b_kernel(m_ref, ct_ref, st_ref, a_ref, o_ref):
    kb = a_ref.shape[2]
    reps = F_W // LANE
    for j in range(kb):
        ar = a_ref[0, 0, j].astype(F32)
        ai = a_ref[0, 1, j].astype(F32)
        ct = jnp.concatenate([ct_ref[j]] * reps, axis=1)
        st = jnp.concatenate([st_ref[j]] * reps, axis=1)
        bri = jnp.concatenate([ar * ct + ai * st, ai * ct - ar * st], axis=0).astype(BF)
        r = jnp.dot(m_ref[...], bri, preferred_element_type=F32)
        o_ref[0, :, j * 2 * F_W:j * 2 * F_W + F_W] = r[:FFT_N2].astype(o_ref.dtype)
        o_ref[0, :, j * 2 * F_W + F_W:(j + 1) * 2 * F_W] = r[FFT_N2:].astype(o_ref.dtype)


def _fft_tables(l):
    n1, n2 = l // FFT_N2, FFT_N2
    ang = lambda a, bb, n: ((a[:, None] * bb[None, :]) % n).astype(F32) * (2.0 * math.pi / n)
    i1 = jnp.arange(n1, dtype=jnp.int32)
    i2 = jnp.arange(n2, dtype=jnp.int32)
    a1 = ang(i1, i1, n1)
    m1 = jnp.concatenate([jnp.cos(a1), -jnp.sin(a1)], axis=0).astype(BF)
    a2 = ang(i2, i2, n2)
    c2, s2 = jnp.cos(a2) * l ** -0.5, jnp.sin(a2) * l ** -0.5
    m2 = jnp.concatenate([jnp.concatenate([c2, s2], axis=1),
                          jnp.concatenate([s2, -c2], axis=1)], axis=0).astype(BF)
    at = ang(i1, i2, l)
    bc = lambda t: jnp.broadcast_to(t[:, :, None], (n1, n2, LANE))
    return m1, m2, bc(jnp.cos(at)), bc(jnp.sin(at))


def _fft(fa_x, m1, m2, ct, st):
    b, l, _ = fa_x.shape
    n1, n2 = l // FFT_N2, FFT_N2
    wide = n2 * F_W
    tn = _tile(wide, 8192, LANE)
    a = pl.pallas_call(
        _fft_a_kernel,
        grid=(b, wide // tn),
        in_specs=[pl.BlockSpec((2 * n1, n1), lambda bi, i: (0, 0)),
                  pl.BlockSpec((1, n1, tn), lambda bi, i: (bi, 0, i))],
        out_specs=pl.BlockSpec((1, 2, n1, tn), lambda bi, i: (bi, 0, 0, i)),
        out_shape=jax.ShapeDtypeStruct((b, 2, n1, wide), BF),
        compiler_params=_params(("parallel", "parallel")),
        name="fft_a",
    )(m1, fa_x.reshape(b, n1, wide))
    kb = 8
    out = pl.pallas_call(
        _fft_b_kernel,
        grid=(b, n1 // kb),
        in_specs=[pl.BlockSpec((2 * n2, 2 * n2), lambda bi, i: (0, 0)),
                  pl.BlockSpec((kb, n2, LANE), lambda bi, i: (i, 0, 0)),
                  pl.BlockSpec((kb, n2, LANE), lambda bi, i: (i, 0, 0)),
                  pl.BlockSpec((1, 2, kb, n2, F_W), lambda bi, i: (bi, 0, i, 0, 0))],
        out_specs=pl.BlockSpec((1, n2, kb * 2 * F_W), lambda bi, i: (bi, 0, i)),
        out_shape=jax.ShapeDtypeStruct((b, n2, n1 * 2 * F_W), BF),
        compiler_params=_params(("parallel", "parallel")),
        name="fft_b",
    )(m2, ct, st, a.reshape(b, 2, n1, n2, F_W))
    return out.reshape(b, l, 2 * F_W)


def _factorizable(l):
    return l % (8 * FFT_N2) == 0


def _position_tables(l):
    return _fft_tables(l) if _factorizable(l) else _dft_tables(l)


def _position_dft(fa_x, tabs):
    return _fft(fa_x, *tabs) if _factorizable(fa_x.shape[1]) else _dft(fa_x, *tabs)


def _gdn_prep_kernel(x_ref, prev_ref, next_ref, ab_ref, cw_ref, alog_ref, dtb_ref,
                     q_ref, k_ref, v_ref, g_ref):
    i = pl.program_id(1)
    last = pl.num_programs(1) - 1
    x = x_ref[0].astype(F32)
    tm = x.shape[0]
    hr = prev_ref.shape[1]
    prev_row = prev_ref[0, hr - 1:hr, :].astype(F32) * (i > 0).astype(F32)
    next_row = next_ref[0, 0:1, :].astype(F32) * (i < last).astype(F32)
    rows = lax.broadcasted_iota(jnp.int32, x.shape, 0)
    x_dn = jnp.where(rows == 0, prev_row, pltpu.roll(x, 1, axis=0))
    x_up = jnp.where(rows == tm - 1, next_row, pltpu.roll(x, tm - 1, axis=0))
    cw = cw_ref[...]
    y = _silu(cw[0:1, :] * x_dn + cw[1:2, :] * x + cw[2:3, :] * x_up)
    for h in range(DN_H):
        qs = y[:, h * DN_DK:(h + 1) * DN_DK]
        ks = y[:, DN_W + h * DN_DK:DN_W + (h + 1) * DN_DK]
        qn = qs * lax.rsqrt(jnp.sum(qs * qs, axis=-1, keepdims=True) + EPS) * (DN_DK ** -0.5)
        kn = ks * lax.rsqrt(jnp.sum(ks * ks, axis=-1, keepdims=True) + EPS)
        q_ref[0, :, h * DN_DK:(h + 1) * DN_DK] = qn.astype(q_ref.dtype)
        k_ref[0, :, h * DN_DK:(h + 1) * DN_DK] = kn.astype(k_ref.dtype)
    v_ref[0] = y[:, 2 * DN_W:].astype(v_ref.dtype)
    a = ab_ref[0]
    cols = lax.broadcasted_iota(jnp.int32, a.shape, 1)
    z = a + dtb_ref[...]
    softplus = jnp.maximum(z, 0.0) + jnp.log1p(jnp.exp(-jnp.abs(z)))
    g = -jnp.exp(alog_ref[...]) * softplus
    g_ref[0] = jnp.where(cols < 2 * DN_H, g, jnp.where(cols < 4 * DN_H, jax.nn.sigmoid(a), 0.0))


def _gdn_prep(qkv, ab, conv_w, a_log, dt_bias):
    b, l, w = qkv.shape
    tm = _tile(l, 512)
    hr = 16
    nb = tm // hr
    pad = jnp.zeros((1, LANE - 2 * DN_H), F32)
    alog = jnp.concatenate([a_log.reshape(1, 2 * DN_H).astype(F32), pad], axis=1)
    dtb = jnp.concatenate([dt_bias.reshape(1, 2 * DN_H).astype(F32), pad], axis=1)
    last_blk = l // hr - 1
    outs = pl.pallas_call(
        _gdn_prep_kernel,
        grid=(b, l // tm),
        in_specs=[pl.BlockSpec((1, tm, w), lambda bi, i: (bi, i, 0)),
                  pl.BlockSpec((1, hr, w), lambda bi, i: (bi, jnp.maximum(i * nb - 1, 0), 0)),
                  pl.BlockSpec((1, hr, w), lambda bi, i: (bi, jnp.minimum((i + 1) * nb, last_blk), 0)),
                  pl.BlockSpec((1, tm, LANE), lambda bi, i: (bi, i, 0)),
                  pl.BlockSpec((3, w), lambda bi, i: (0, 0)),
                  pl.BlockSpec((1, LANE), lambda bi, i: (0, 0)),
                  pl.BlockSpec((1, LANE), lambda bi, i: (0, 0))],
        out_specs=[pl.BlockSpec((1, tm, DN_W), lambda bi, i: (bi, i, 0)),
                   pl.BlockSpec((1, tm, DN_W), lambda bi, i: (bi, i, 0)),
                   pl.BlockSpec((1, tm, DN_W), lambda bi, i: (bi, i, 0)),
                   pl.BlockSpec((1, tm, LANE), lambda bi, i: (bi, i, 0))],
        out_shape=[jax.ShapeDtypeStruct((b, l, DN_W), BF)] * 3 + [jax.ShapeDtypeStruct((b, l, LANE), F32)],
        compiler_params=_params(("parallel", "parallel")),
        name="gdn_prep",
    )(qkv, qkv, qkv, ab, conv_w.astype(F32), alog, dtb)
    return outs


def _nt_dot(a, b):
    return lax.dot_general(a, b, (((1,), (1,)), ((), ())), preferred_element_type=F32)


def _gdn_local_kernel(q_ref, k_ref, v_ref, g_ref, a1_ref, a2_ref, u_ref, egl_ref, *, cg):
    c = DN_CHUNK
    ri = lax.broadcasted_iota(jnp.int32, (c, c), 0)
    ci = lax.broadcasted_iota(jnp.int32, (c, c), 1)
    eye = (ri == ci).astype(F32)
    lvl_masks = []
    bsz = 1
    while bsz < c:
        lvl_masks.append((ri // (2 * bsz) == ci // (2 * bsz)) & (ri // bsz != ci // bsz))
        bsz *= 2
    incl = (ri >= ci, ri <= ci)
    strict = (ri > ci, ri < ci)
    last = (c - 1, 0)

    systems = []
    for cc in range(cg):
        rows = slice(cc * c, (cc + 1) * c)
        ga = g_ref[0, rows, :]
        heads = []
        for h in range(DN_H):
            cols = slice(h * DN_DK, (h + 1) * DN_DK)
            q, k, v = q_ref[0, rows, cols], k_ref[0, rows, cols], v_ref[0, rows, cols]
            heads.append((q, k, v, _nt_dot(k, k), _nt_dot(q, k)))
        for d in range(2):
            gcum = jnp.dot(incl[d].astype(F32), ga, precision=HIGHEST, preferred_element_type=F32)
            gcum_t = jnp.concatenate([gcum, jnp.zeros_like(gcum)], axis=0).T
            for h in range(DN_H):
                q, k, v, kk, qk = heads[h]
                col = d * DN_H + h
                gc = gcum[:, col:col + 1]
                gr = gcum_t[col:col + 1, :c]
                bc = ga[:, 2 * DN_H + col:2 * DN_H + col + 1]
                gl = gcum[last[d]:last[d] + 1, col:col + 1]
                decay = jnp.where(incl[d], jnp.exp(jnp.where(incl[d], gc - gr, 0.0)), 0.0)
                a_m = jnp.where(strict[d], bc * kk * decay, 0.0)
                systems.append(dict(idx=(d, cc, h), q=q.astype(F32), k=k.astype(F32), v=v.astype(F32), gc=gc, bc=bc,
                                    gl=gl, eg=jnp.exp(gc), qk=qk * decay, a_m=a_m,
                                    t=eye - jnp.where(lvl_masks[0], a_m, 0.0)))
    for lm in lvl_masks[1:]:
        inner = [_dot(jnp.where(lm, s["a_m"], 0.0), s["t"]) for s in systems]
        for s, m in zip(systems, inner):
            s["t"] = s["t"] - _dot(s["t"], m)
    sols = [_dot(s["t"], jnp.concatenate([(s["bc"] * s["eg"]) * s["k"], s["bc"] * s["v"]], axis=1))
            for s in systems]
    for s, sol in zip(systems, sols):
        d, cc, h = s["idx"]
        a1_ref[0, d, cc, h, :c, :] = sol[:, :DN_DK].astype(BF)
        a1_ref[0, d, cc, h, c:, :] = (s["q"] * s["eg"]).astype(BF)
        a2_ref[0, d, cc, h, :c, :] = s["qk"].astype(BF)
        a2_ref[0, d, cc, h, c:, :] = (s["k"] * jnp.exp(s["gl"] - s["gc"])).T.astype(BF)
        u_ref[0, d, cc, h] = sol[:, DN_DK:].astype(BF)
        egl_ref[0, d, cc, h:h + 1, :] = jnp.broadcast_to(jnp.exp(s["gl"]), (1, LANE))


def _gdn_local(qn, kn, v, gact):
    b, l, _ = qn.shape
    n = l // DN_CHUNK
    cg = 2 if n % 2 == 0 else 1
    c = DN_CHUNK
    row = lambda w: pl.BlockSpec((1, cg * c, w), lambda bi, i: (bi, i, 0))
    out = lambda r, w: pl.BlockSpec((1, 2, cg, DN_H, r, w), lambda bi, i: (bi, 0, i, 0, 0, 0))
    return pl.pallas_call(
        functools.partial(_gdn_local_kernel, cg=cg),
        grid=(b, n // cg),
        in_specs=[row(DN_W), row(DN_W), row(DN_W), row(LANE)],
        out_specs=[out(2 * c, DN_DK), out(c + DN_DK, c), out(c, DN_DK),
                   pl.BlockSpec((1, 2, cg, DN_H, LANE), lambda bi, i: (bi, 0, i, 0, 0))],
        out_shape=[jax.ShapeDtypeStruct((b, 2, n, DN_H, 2 * c, DN_DK), BF),
                   jax.ShapeDtypeStruct((b, 2, n, DN_H, c + DN_DK, c), BF),
                   jax.ShapeDtypeStruct((b, 2, n, DN_H, c, DN_DK), BF),
                   jax.ShapeDtypeStruct((b, 2, n, DN_H, LANE), F32)],
        compiler_params=_params(("parallel", "parallel")),
        name="gdn_local",
    )(qn, kn, v, gact)


def _gdn_scan_kernel(a1f, a2f, uf, ef, a1b, a2b, ub, eb, s0_ref, of_ref, ob_ref, s_ref, *, bg):
    j = pl.program_id(1)

    @pl.when(j == 0)
    def _():
        s_ref[...] = s0_ref[...]

    c = DN_CHUNK
    dirs = ((a1f, a2f, uf, ef, of_ref), (a1b, a2b, ub, eb, ob_ref))
    chains = [(bb, d, h) for bb in range(bg) for d in range(2) for h in range(DN_H)]
    st = [s_ref[bb, d, h] for bb, d, h in chains]
    r1 = [jnp.dot(dirs[d][0][bb, 0, 0, h], s.astype(BF), preferred_element_type=F32)
          for (bb, d, h), s in zip(chains, st)]
    un = [dirs[d][2][bb, 0, 0, h].astype(F32) - r[:c] for (bb, d, h), r in zip(chains, r1)]
    r2 = [jnp.dot(dirs[d][1][bb, 0, 0, h], u.astype(BF), preferred_element_type=F32)
          for (bb, d, h), u in zip(chains, un)]
    for (bb, d, h), s, ra, rb in zip(chains, st, r1, r2):
        dirs[d][4][bb, :, h * DN_DK:(h + 1) * DN_DK] = (ra[c:] + rb[:c]).astype(of_ref.dtype)
        s_ref[bb, d, h] = s * dirs[d][3][bb, 0, 0, h:h + 1, :] + rb[c:]


def _gdn_scan(a1, a2, u, egl, s0):
    b, _, n, _, _, _ = a1.shape
    c = DN_CHUNK
    bg = max(g for g in (1, 2, 4) if b % g == 0)
    fwd = lambda bi, j: (bi, 0, j, 0, 0, 0)
    bwd = lambda bi, j: (bi, 1, n - 1 - j, 0, 0, 0)
    blk = lambda a, m: pl.BlockSpec((bg, 1, 1) + a.shape[3:], m)
    eblk = lambda m: pl.BlockSpec((bg, 1, 1, DN_H, LANE), lambda bi, j: m(bi, j)[:5])
    st = pl.BlockSpec((bg, 2, DN_H, DN_DK, DN_DK), lambda bi, j: (bi, 0, 0, 0, 0))
    return pl.pallas_call(
        functools.partial(_gdn_scan_kernel, bg=bg),
        grid=(b // bg, n),
        in_specs=[blk(a1, fwd), blk(a2, fwd), blk(u, fwd), eblk(fwd),
                  blk(a1, bwd), blk(a2, bwd), blk(u, bwd), eblk(bwd), st],
        out_specs=[pl.BlockSpec((bg, c, DN_W), lambda bi, j: (bi, j, 0)),
                   pl.BlockSpec((bg, c, DN_W), lambda bi, j: (bi, n - 1 - j, 0)), st],
        out_shape=[jax.ShapeDtypeStruct((b, n * c, DN_W), BF), jax.ShapeDtypeStruct((b, n * c, DN_W), BF),
                   jax.ShapeDtypeStruct((b, 2, DN_H, DN_DK, DN_DK), F32)],
        compiler_params=_params(("parallel", "arbitrary")),
        name="gdn_scan",
    )(a1, a2, u, egl, a1, a2, u, egl, s0)


def _mla_prep_kernel(m_ref, qn_ref, kvn_ref, wq_ref, wqr_ref, wk_ref, wv_ref, cos_ref, sin_ref,
                     q_ref, k_ref, v_ref):
    m = m_ref[0].astype(F32)
    cq = m[:, :M_QL]
    ckv = m[:, M_QL:M_QL + M_KVL]
    kpa = m[:, M_QL + M_KVL:M_QL + M_KVL + LANE]
    kpb = m[:, M_QL + M_KVL + LANE:]
    hq = (cq * lax.rsqrt(jnp.mean(cq * cq, axis=-1, keepdims=True) + EPS) * qn_ref[...]).astype(BF)
    hkv = (ckv * lax.rsqrt(jnp.mean(ckv * ckv, axis=-1, keepdims=True) + EPS) * kvn_ref[...]).astype(BF)
    cos = cos_ref[...]
    sin = sin_ref[...]
    scale = (M_NOPE + M_ROPE) ** -0.5 * math.log2(math.e)
    q_all = jnp.dot(hq, wq_ref[...], preferred_element_type=F32)
    q_rot = jnp.dot(hq, wqr_ref[...], preferred_element_type=F32)
    k_all = jnp.dot(hkv, wk_ref[...], preferred_element_type=F32)
    v_all_t = _nt_dot(wv_ref[...], hkv)
    kpe = kpa * cos + kpb * sin
    rows = lax.broadcasted_iota(jnp.int32, (M_HP, hkv.shape[0]), 0)
    one_row = jnp.where(rows == M_V, 1.0, 0.0)
    for h in range(M_H):
        sl = slice(h * M_HP, (h + 1) * M_HP)
        q_ref[0, h] = ((q_all[:, sl] * cos + q_rot[:, sl] * sin) * scale).astype(q_ref.dtype)
        k_ref[0, h] = (k_all[:, sl] + kpe).astype(k_ref.dtype)
        v_ref[0, h] = (v_all_t[sl, :] + one_row).astype(v_ref.dtype)


def _mla_prep(mla_in, q_norm, kv_norm, wq, wqr, wk, wv, cos_t, sin_t):
    b, l, w = mla_in.shape
    tm = _tile(l, 512)
    full = lambda a: pl.BlockSpec(a.shape, lambda bi, i: (0,) * a.ndim)
    wvt = wv.T
    return pl.pallas_call(
        _mla_prep_kernel,
        grid=(b, l // tm),
        in_specs=[pl.BlockSpec((1, tm, w), lambda bi, i: (bi, i, 0)),
                  pl.BlockSpec((1, M_QL), lambda bi, i: (0, 0)),
                  pl.BlockSpec((1, M_KVL), lambda bi, i: (0, 0)),
                  full(wq), full(wqr), full(wk), full(wvt),
                  pl.BlockSpec((tm, LANE), lambda bi, i: (i, 0)),
                  pl.BlockSpec((tm, LANE), lambda bi, i: (i, 0))],
        out_specs=[pl.BlockSpec((1, M_H, tm, M_HP), lambda bi, i: (bi, 0, i, 0))] * 2
                  + [pl.BlockSpec((1, M_H, M_HP, tm), lambda bi, i: (bi, 0, 0, i))],
        out_shape=[jax.ShapeDtypeStruct((b, M_H, l, M_HP), BF)] * 2
                  + [jax.ShapeDtypeStruct((b, M_H, M_HP, l), BF)],
        compiler_params=_params(("parallel", "parallel")),
        name="mla_prep",
    )(mla_in, q_norm.reshape(1, M_QL).astype(F32), kv_norm.reshape(1, M_KVL).astype(F32),
      wq, wqr, wk, wvt, cos_t, sin_t)


def _attn_kernel(q_ref, *refs, n_src):
    k_refs, v_refs, o_ref = refs[:n_src], refs[n_src:2 * n_src], refs[2 * n_src]
    tq = q_ref.shape[2]
    sub = min(tq, ATT_SUB)
    units = [(hh, r0) for hh in range(2) for r0 in range(0, tq, sub)]

    def scores(u):
        hh, r0 = u
        return [_nt_dot(k_r[0, hh], q_ref[0, hh, r0:r0 + sub, :]) for k_r in k_refs]

    def finish(u, s):
        hh, r0 = u
        m = s[0].max(axis=0, keepdims=True)
        for si in s[1:]:
            m = jnp.maximum(m, si.max(axis=0, keepdims=True))
        acc = None
        for si, v_r in zip(s, v_refs):
            nk = si.shape[0]
            step = nk // ATT_KSPLIT if nk >= 1024 else nk
            for k0 in range(0, nk, step):
                part = jnp.dot(v_r[0, hh, :ATT_VROWS, k0:k0 + step], jnp.exp2(si[k0:k0 + step] - m).astype(BF),
                               preferred_element_type=F32)
                acc = part if acc is None else acc + part
        o = acc[:M_V] / acc[M_V:M_V + 1]
        o_ref[0, r0:r0 + sub, hh * M_V:(hh + 1) * M_V] = o.T.astype(o_ref.dtype)

    s_cur = scores(units[0])
    for i, u in enumerate(units):
        s_nxt = scores(units[i + 1]) if i + 1 < len(units) else None
        finish(u, s_cur)
        s_cur = s_nxt


def _attention(q, ks, vs):
    b, _, l, _ = q.shape
    tq = _tile(l, 2 * ATT_SUB)
    n_src = len(ks)
    kv_spec = lambda a: pl.BlockSpec((1, 2) + a.shape[2:], lambda bi, hp, i: (bi, hp, 0, 0))
    return pl.pallas_call(
        functools.partial(_attn_kernel, n_src=n_src),
        grid=(b, M_H // 2, l // tq),
        in_specs=[pl.BlockSpec((1, 2, tq, M_HP), lambda bi, hp, i: (bi, hp, i, 0))]
                 + [kv_spec(a) for a in ks] + [kv_spec(a) for a in vs],
        out_specs=pl.BlockSpec((1, tq, 2 * M_V), lambda bi, hp, i: (bi, i, hp)),
        out_shape=jax.ShapeDtypeStruct((b, l, M_H * M_V), BF),
        compiler_params=_params(("parallel", "parallel", "arbitrary")),
        name="attention",
    )(q, *ks, *vs)


def _merge_kernel(pq_ref, faz_ref, of_ref, ob_ref, dnz_ref, om_ref, mz_ref, gl_ref, x_ref, gm_ref, gp_ref,
                  fw_ref, cc_ref, sc_ref, dnn_ref, wb_ref, wo_ref, o_ref):
    pq = pq_ref[0]
    faz = faz_ref[0].astype(F32)
    ya = []
    for g in range(F_GROUPS):
        sl = slice(g * F_GD, (g + 1) * F_GD)
        spec = (jnp.dot(pq[:, sl], cc_ref[...], preferred_element_type=F32)
                - jnp.dot(pq[:, F_W + g * F_GD:F_W + (g + 1) * F_GD], sc_ref[...], preferred_element_type=F32))
        t = jnp.dot(spec.astype(BF), fw_ref[g], preferred_element_type=F32)
        ya.append((t * _silu(faz[:, sl])).astype(BF))
    ya = jnp.concatenate(ya, axis=1)
    osum = of_ref[0].astype(F32) + ob_ref[0].astype(F32)
    dnz = dnz_ref[0].astype(F32)
    yb = []
    for h in range(DN_H):
        sl = slice(h * DN_DK, (h + 1) * DN_DK)
        seg = osum[:, sl]
        nrm = seg * lax.rsqrt(jnp.mean(seg * seg, axis=-1, keepdims=True) + EPS) * dnn_ref[...]
        yb.append((nrm * _silu(dnz[:, sl])).astype(BF))
    yb = jnp.concatenate(yb, axis=1)
    yc = (om_ref[0].astype(F32) * _silu(mz_ref[0].astype(F32))).astype(BF)
    merged = None
    for idx, yy in enumerate((ya, yb, yc)):
        gate = jax.nn.sigmoid(gl_ref[0, :, idx * D:(idx + 1) * D].astype(F32))
        term = gate * jnp.dot(yy, wb_ref[idx], preferred_element_type=F32)
        merged = term if merged is None else merged + term
    y = jnp.dot(merged.astype(BF), wo_ref[...], preferred_element_type=F32)
    yn = y * lax.rsqrt(jnp.mean(y * y, axis=-1, keepdims=True) + EPS) * gp_ref[...]
    o_ref[0] = x_ref[0] + gm_ref[0] * yn


def _merge(pq, fa_z, o_f, o_b, dn_z, o_m, mla_z, gate_logits, x, gate_mod, g_post, fw, cc, sc, dn_norm, wb, wo,
           per_batch_mod):
    b, l, _ = x.shape
    tm = _tile(l, 512)
    row = lambda w: pl.BlockSpec((1, tm, w), lambda bi, i: (bi, i, 0))
    full = lambda a: pl.BlockSpec(a.shape, lambda bi, i: (0,) * a.ndim)
    mod_map = (lambda bi, i: (bi, 0, 0)) if per_batch_mod else (lambda bi, i: (0, 0, 0))
    gp = g_post.reshape(1, D).astype(F32)
    dnn = dn_norm.reshape(1, DN_DK).astype(F32)
    return pl.pallas_call(
        _merge_kernel,
        grid=(b, l // tm),
        in_specs=[row(2 * F_W), row(F_W), row(DN_W), row(DN_W), row(DN_W), row(512), row(512), row(3 * D), row(D),
                  pl.BlockSpec((1, 1, D), mod_map), full(gp), full(fw), full(cc), full(sc), full(dnn),
                  full(wb), full(wo)],
        out_specs=row(D),
        out_shape=jax.ShapeDtypeStruct((b, l, D), F32),
        compiler_params=_params(("parallel", "parallel")),
        name="merge",
    )(pq, fa_z, o_f, o_b, dn_z, o_m, mla_z, gate_logits, x, gate_mod, gp, fw, cc, sc, dnn, wb, wo)


def _rot_cols(w):
    q = M_ROPE // 4
    return jnp.concatenate([-w[..., q:2 * q], w[..., :q], -w[..., 3 * q:], w[..., 2 * q:3 * q]], axis=-1)


def _layer_weights(w_in, mla_w_uq, mla_w_ukv):
    col = lambda n: w_in[:, _OFF[n][0]:_OFF[n][1]]
    zeros = lambda n: jnp.zeros((D, n), F32)
    w_qkv = jnp.concatenate([col("dn_q"), col("dn_k"), col("dn_v")], axis=1)
    w_ab = jnp.concatenate([col("dn_ab"), zeros(LANE - 4 * DN_H)], axis=1)
    kpe = col("kpe")
    rest = LANE - M_NOPE - M_ROPE
    kpe_a = jnp.concatenate([zeros(M_NOPE), kpe, zeros(rest)], axis=1)
    kpe_b = jnp.concatenate([zeros(M_NOPE), _rot_cols(kpe), zeros(rest)], axis=1)
    w_mla = jnp.concatenate([col("cq"), col("ckv"), kpe_a, kpe_b], axis=1)
    grp1 = [col("fa_x"), col("fa_z"), w_qkv, col("dn_z"), w_ab]
    grp2 = [w_mla, col("mla_z"), col("gate")]
    uq = mla_w_uq.reshape(M_QL, M_H, M_NOPE + M_ROPE)
    zq = jnp.zeros((M_QL, M_H, rest), F32)
    wq = jnp.concatenate([uq, zq], axis=-1).reshape(M_QL, M_H * M_HP)
    wqr = jnp.concatenate([jnp.zeros((M_QL, M_H, M_NOPE), F32), _rot_cols(uq[..., M_NOPE:]), zq],
                          axis=-1).reshape(M_QL, M_H * M_HP)
    ukv = mla_w_ukv.reshape(M_KVL, M_H, M_NOPE + M_V)
    zk = jnp.zeros((M_KVL, M_H, M_HP - M_NOPE), F32)
    wk = jnp.concatenate([ukv[..., :M_NOPE], zk], axis=-1).reshape(M_KVL, M_H * M_HP)
    wv = jnp.concatenate([ukv[..., M_NOPE:], jnp.zeros((M_KVL, M_H, M_HP - M_V), F32)],
                         axis=-1).reshape(M_KVL, M_H * M_HP)
    cast = lambda ws: [w.astype(BF) for w in ws]
    return cast(grp1), cast(grp2), cast([wq, wqr, wk, wv])


def _rope_tables(l, rotary):
    ones = jnp.ones((l, M_NOPE), F32)
    pad = jnp.zeros((l, LANE - M_NOPE - M_ROPE), F32)
    if not rotary:
        return (jnp.concatenate([ones, jnp.ones((l, M_ROPE), F32), pad], axis=1),
                jnp.zeros((l, LANE), F32))
    pos = jnp.arange(l, dtype=jnp.int32)
    row = (pos // GRID_W).astype(F32)
    colp = (pos % GRID_W).astype(F32)
    n_freq = M_ROPE // 4
    inv = ROPE_BASE ** (-jnp.arange(n_freq, dtype=F32) / n_freq)
    ang_r = row[:, None] * inv
    ang_c = colp[:, None] * inv
    ang = jnp.concatenate([ang_r, ang_r, ang_c, ang_c], axis=-1)
    return (jnp.concatenate([ones, jnp.cos(ang), pad], axis=1),
            jnp.concatenate([jnp.zeros((l, M_NOPE), F32), jnp.sin(ang), pad], axis=1))


def _chan_tables():
    j = jnp.arange(F_GD, dtype=jnp.int32)
    a = ((j[:, None] * j[None, :]) % F_GD).astype(F32) * (2.0 * math.pi / F_GD)
    nrm = F_GD ** -0.5
    return (jnp.cos(a) * nrm).astype(BF), (jnp.sin(a) * nrm).astype(BF)


def _layer(x, ctx, mod, p, tables, need_ctx_out):
    b = x.shape[0]
    shift, scale, gate = (mod[:b, None, i * D:(i + 1) * D] for i in range(3))
    shift_c, scale_c, gate_c = (mod[b:b + 1, None, i * D:(i + 1) * D] for i in range(3))
    grp1, grp2, (wq, wqr, wk, wv) = _layer_weights(p["w_in"], p["mla_w_uq"], p["mla_w_ukv"])
    dts1 = [BF, BF, BF, BF, F32]
    dts2 = [BF, BF, BF]

    def project(t, sc, sh, per_batch):
        a = _inproj(t, sc, sh, p["g_pre"], grp1, dts1, per_batch)
        bb = _inproj(t, sc, sh, p["g_pre"], grp2, dts2, per_batch)
        return a, bb

    (fa_x, fa_z, qkv, dn_z, ab), (mla_in, mla_z, gl) = project(x, scale, shift, True)
    (fa_xc, fa_zc, qkvc, dn_zc, abc), (mla_inc, mla_zc, glc) = project(ctx, scale_c, shift_c, False)

    qn, kn, vv, gact = _gdn_prep(qkv, ab, p["dn_conv"], p["dn_a_log"], p["dn_dt_bias"])
    qnc, knc, vvc, gactc = _gdn_prep(qkvc, abc, p["dn_conv"], p["dn_a_log"], p["dn_dt_bias"])
    s0 = jnp.zeros((b, 2, DN_H, DN_DK, DN_DK), F32)
    ofc, obc, s_ctx = _gdn_scan(*_gdn_local(qnc, knc, vvc, gactc), s0)
    of, ob, _ = _gdn_scan(*_gdn_local(qn, kn, vv, gact), s_ctx)

    q_m, k_m, v_m = _mla_prep(mla_in, p["mla_q_norm"], p["mla_kv_norm"], wq, wqr, wk, wv, *tables["rope"])
    qc_m, kc_m, vc_m = _mla_prep(mla_inc, p["mla_q_norm"], p["mla_kv_norm"], wq, wqr, wk, wv, *tables["rope_c"])
    o_m = _attention(q_m, [k_m, kc_m], [v_m, vc_m])

    pq = _position_dft(fa_x, tables["dft"])
    fw = p["f_w"].astype(BF)
    wb = p["w_branch"].astype(BF)
    wo = p["w_out"].astype(BF)
    cc, sc = tables["chan"]
    x_new = _merge(pq, fa_z, of, ob, dn_z, o_m, mla_z, gl, x, gate, p["g_post"], fw, cc, sc, p["dn_norm"],
                   wb, wo, True)
    ctx_new = ctx
    if need_ctx_out:
        pqc = _position_dft(fa_xc, tables["dft_c"])
        oc_m = _attention(qc_m, [kc_m], [vc_m])
        ctx_new = _merge(pqc, fa_zc, ofc, obc, dn_zc, oc_m, mla_zc, glc, ctx, gate_c, p["g_post"], fw, cc, sc,
                         p["dn_norm"], wb, wo, False)
    return x_new, ctx_new


def kernel(x, c, ctx, c_ctx, w_mod, b_mod, g_pre, g_post, w_in, f_w, dn_conv, dn_a_log, dn_dt_bias, dn_norm,
           mla_q_norm, mla_w_uq, mla_kv_norm, mla_w_ukv, w_branch, w_out):
    b, l, _ = x.shape
    lc = ctx.shape[1]
    depth = w_mod.shape[0]
    rows = -(-(b + 1) // 8) * 8
    c_all = jnp.concatenate([c, c_ctx[None, :], jnp.zeros((rows - b - 1, D), F32)], axis=0)
    tables = {"dft": _position_tables(l), "dft_c": _position_tables(lc), "chan": _chan_tables(),
              "rope": _rope_tables(l, True), "rope_c": _rope_tables(lc, False)}
    per_layer = dict(g_pre=g_pre, g_post=g_post, w_in=w_in, f_w=f_w, dn_conv=dn_conv, dn_a_log=dn_a_log,
                     dn_dt_bias=dn_dt_bias, dn_norm=dn_norm, mla_q_norm=mla_q_norm, mla_w_uq=mla_w_uq,
                     mla_kv_norm=mla_kv_norm, mla_w_ukv=mla_w_ukv, w_branch=w_branch, w_out=w_out)
    for li in range(depth):
        p = {k: v[li] for k, v in per_layer.items()}
        mod = _modulation(c_all, w_mod[li], b_mod[li])
        x, ctx = _layer(x, ctx, mod, p, tables, need_ctx_out=(li < depth - 1))
    return x
```

```python
import functools
import math

import jax
import jax.numpy as jnp
from jax import lax
from jax.experimental import pallas as pl
from jax.experimental.pallas import tpu as pltpu

F32 = jnp.float32
BF = jnp.bfloat16
HIGHEST = lax.Precision.HIGHEST

D = 1024
EPS = 1e-6
GRID_W = 64
F_GROUPS = 4
F_GD = 128
F_W = 512
DN_H = 4
DN_DK = 128
DN_W = 512
DN_CHUNK = 64
DN_PAIRS = DN_H // 2
DN_PW = 2 * DN_DK
M_H = 8
M_QL = 384
M_KVL = 256
M_NOPE = 64
M_ROPE = 32
M_V = 64
M_HP = 128
ROPE_BASE = 10000.0
LANE = 128
VMEM_LIMIT = 56 * 1024 * 1024
ATT_SUB = 512
ATT_KT = 4096
ATT_VROWS = 80

_OFF = {}
_o = 0
for _n, _w in (("fa_x", 512), ("fa_z", 512), ("dn_q", 512), ("dn_k", 512), ("dn_v", 512), ("dn_z", 512),
               ("dn_ab", 16), ("cq", M_QL), ("ckv", M_KVL), ("kpe", M_ROPE), ("mla_z", 512), ("gate", 3 * D)):
    _OFF[_n] = (_o, _o + _w)
    _o += _w


def _tile(n, pref, mult=16):
    t = min(n, pref)
    while t > mult and (n % t or t % mult):
        t -= mult
    assert n % t == 0, (n, pref)
    return t


def _params(sem):
    return pltpu.CompilerParams(dimension_semantics=sem, vmem_limit_bytes=VMEM_LIMIT)


def _sigmoid(v):
    return 0.5 * jnp.tanh(0.5 * v) + 0.5


def _silu(v):
    return v * _sigmoid(v)


def _dot(a, b):
    return jnp.dot(a.astype(BF), b.astype(BF), preferred_element_type=F32)


def _mod_kernel(c_ref, w_ref, b_ref, o_ref):
    c = c_ref[...]
    o_ref[...] = jnp.dot(_silu(c), w_ref[...], precision=HIGHEST, preferred_element_type=F32) + b_ref[...]


def _modulation(c_all, w_mod, b_mod):
    r = c_all.shape[0]
    tn = 512
    return pl.pallas_call(
        _mod_kernel,
        grid=(3 * D // tn,),
        in_specs=[pl.BlockSpec((r, D), lambda n: (0, 0)),
                  pl.BlockSpec((D, tn), lambda n: (0, n)),
                  pl.BlockSpec((1, tn), lambda n: (0, n))],
        out_specs=pl.BlockSpec((r, tn), lambda n: (0, n)),
        out_shape=jax.ShapeDtypeStruct((r, 3 * D), F32),
        compiler_params=_params(("arbitrary",)),
        name="modulation",
    )(c_all, w_mod, b_mod.reshape(1, 3 * D))


def _inproj_kernel(x_ref, sc_ref, sh_ref, g_ref, *refs, n_w):
    w_refs, o_refs = refs[:n_w], refs[n_w:]
    x = x_ref[0]
    y = x * lax.rsqrt(jnp.mean(x * x, axis=-1, keepdims=True) + EPS) * g_ref[...]
    hb = (y * (1.0 + sc_ref[0]) + sh_ref[0]).astype(BF)
    for w_ref, o_ref in zip(w_refs, o_refs):
        n = w_ref.shape[1]
        step = 512 if n % 512 == 0 else n
        for c0 in range(0, n, step):
            o_ref[0, :, c0:c0 + step] = jnp.dot(
                hb, w_ref[:, c0:c0 + step], preferred_element_type=F32).astype(o_ref.dtype)


def _inproj(x, scale, shift, g_pre, weights, out_dtypes, per_batch_mod):
    b, l, _ = x.shape
    tm = _tile(l, 1024)
    mod_map = (lambda bi, i: (bi, 0, 0)) if per_batch_mod else (lambda bi, i: (0, 0, 0))
    in_specs = [pl.BlockSpec((1, tm, D), lambda bi, i: (bi, i, 0)),
                pl.BlockSpec((1, 1, D), mod_map),
                pl.BlockSpec((1, 1, D), mod_map),
                pl.BlockSpec((1, D), lambda bi, i: (0, 0))]
    in_specs += [pl.BlockSpec(w.shape, lambda bi, i: (0, 0)) for w in weights]
    out_specs = [pl.BlockSpec((1, tm, w.shape[1]), lambda bi, i: (bi, i, 0)) for w in weights]
    out_shape = [jax.ShapeDtypeStruct((b, l, w.shape[1]), dt) for w, dt in zip(weights, out_dtypes)]
    return pl.pallas_call(
        functools.partial(_inproj_kernel, n_w=len(weights)),
        grid=(b, l // tm),
        in_specs=in_specs, out_specs=out_specs, out_shape=out_shape,
        compiler_params=_params(("parallel", "parallel")),
        name="inproj",
    )(x, scale, shift, g_pre.reshape(1, D), *weights)


def _dft_kernel(c_ref, s_ref, x_ref, o_ref, accp, accq):
    k = pl.program_id(2)

    @pl.when(k == 0)
    def _():
        accp[...] = jnp.zeros_like(accp)
        accq[...] = jnp.zeros_like(accq)

    xk = x_ref[0]
    accp[...] += jnp.dot(c_ref[...], xk, preferred_element_type=F32)
    accq[...] += jnp.dot(s_ref[...], xk, preferred_element_type=F32)

    @pl.when(k == pl.num_programs(2) - 1)
    def _():
        o_ref[0, :, :F_W] = accp[...].astype(o_ref.dtype)
        o_ref[0, :, F_W:] = accq[...].astype(o_ref.dtype)


def _dft_tables(l):
    blk = min(l, 64)
    j1 = jnp.arange(l // blk, dtype=jnp.int32)[:, None] * blk
    j2 = jnp.arange(blk, dtype=jnp.int32)[:, None]
    k = jnp.arange(l, dtype=jnp.int32)[None, :]
    w = 2.0 * math.pi / l
    a = ((j1 * k) % l).astype(F32) * w
    bb = ((j2 * k) % l).astype(F32) * w
    ca, sa, cb, sb = jnp.cos(a), jnp.sin(a), jnp.cos(bb), jnp.sin(bb)
    nrm = l ** -0.5
    cos = (ca[:, None, :] * cb[None] - sa[:, None, :] * sb[None]).reshape(l, l) * nrm
    sin = (sa[:, None, :] * cb[None] + ca[:, None, :] * sb[None]).reshape(l, l) * nrm
    return cos.astype(BF), sin.astype(BF)


def _dft(fa_x, cos_t, sin_t):
    b, l, _ = fa_x.shape
    tm = _tile(l, 1024)
    tk = _tile(l, 1024)
    return pl.pallas_call(
        _dft_kernel,
        grid=(b, l // tm, l // tk),
        in_specs=[pl.BlockSpec((tm, tk), lambda bi, i, k: (i, k)),
                  pl.BlockSpec((tm, tk), lambda bi, i, k: (i, k)),
                  pl.BlockSpec((1, tk, F_W), lambda bi, i, k: (bi, k, 0))],
        out_specs=pl.BlockSpec((1, tm, 2 * F_W), lambda bi, i, k: (bi, i, 0)),
        out_shape=jax.ShapeDtypeStruct((b, l, 2 * F_W), BF),
        scratch_shapes=[pltpu.VMEM((tm, F_W), F32), pltpu.VMEM((tm, F_W), F32)],
        compiler_params=_params(("parallel", "parallel", "arbitrary")),
        name="dft",
    )(cos_t, sin_t, fa_x)


FFT_N2 = 64


def _fft_a_kernel(m_ref, x_ref, o_ref):
    n1 = x_ref.shape[1]
    r = jnp.dot(m_ref[...], x_ref[0], preferred_element_type=F32)
    o_ref[0, 0] = r[:n1].astype(o_ref.dtype)
    o_ref[0, 1] = r[n1:].astype(o_ref.dtype)


def _fft_b_kernel(m_ref, ct_ref, st_ref, a_ref, o_ref):
    kb = a_ref.shape[2]
    reps = F_W // LANE
    for j in range(kb):
        ar = a_ref[0, 0, j].astype(F32)
        ai = a_ref[0, 1, j].astype(F32)
        ct = jnp.concatenate([ct_ref[j]] * reps, axis=1)
        st = jnp.concatenate([st_ref[j]] * reps, axis=1)
        bri = jnp.concatenate([ar * ct + ai * st, ai * ct - ar * st], axis=0).astype(BF)
        r = jnp.dot(m_ref[...], bri, preferred_element_type=F32)
        o_ref[0, :, j * 2 * F_W:j * 2 * F_W + F_W] = r[:FFT_N2].astype(o_ref.dtype)
        o_ref[0, :, j * 2 * F_W + F_W:(j + 1) * 2 * F_W] = r[FFT_N2:].astype(o_ref.dtype)


def _fft_tables(l):
    n1, n2 = l // FFT_N2, FFT_N2
    ang = lambda a, bb, n: ((a[:, None] * bb[None, :]) % n).astype(F32) * (2.0 * math.pi / n)
    i1 = jnp.arange(n1, dtype=jnp.int32)
    i2 = jnp.arange(n2, dtype=jnp.int32)
    a1 = ang(i1, i1, n1)
    m1 = jnp.concatenate([jnp.cos(a1), -jnp.sin(a1)], axis=0).astype(BF)
    a2 = ang(i2, i2, n2)
    c2, s2 = jnp.cos(a2) * l ** -0.5, jnp.sin(a2) * l ** -0.5
    m2 = jnp.concatenate([jnp.concatenate([c2, s2], axis=1),
                          jnp.concatenate([s2, -c2], axis=1)], axis=0).astype(BF)
    at = ang(i1, i2, l)
    bc = lambda t: jnp.broadcast_to(t[:, :, None], (n1, n2, LANE))
    return m1, m2, bc(jnp.cos(at)), bc(jnp.sin(at))


def _fft(fa_x, m1, m2, ct, st):
    b, l, _ = fa_x.shape
    n1, n2 = l // FFT_N2, FFT_N2
    wide = n2 * F_W
    tn = _tile(wide, 8192, LANE)
    a = pl.pallas_call(
        _fft_a_kernel,
        grid=(b, wide // tn),
        in_specs=[pl.BlockSpec((2 * n1, n1), lambda bi, i: (0, 0)),
                  pl.BlockSpec((1, n1, tn), lambda bi, i: (bi, 0, i))],
        out_specs=pl.BlockSpec((1, 2, n1, tn), lambda bi, i: (bi, 0, 0, i)),
        out_shape=jax.ShapeDtypeStruct((b, 2, n1, wide), BF),
        compiler_params=_params(("parallel", "parallel")),
        name="fft_a",
    )(m1, fa_x.reshape(b, n1, wide))
    kb = 8
    out = pl.pallas_call(
        _fft_b_kernel,
        grid=(b, n1 // kb),
        in_specs=[pl.BlockSpec((2 * n2, 2 * n2), lambda bi, i: (0, 0)),
                  pl.BlockSpec((kb, n2, LANE), lambda bi, i: (i, 0, 0)),
                  pl.BlockSpec((kb, n2, LANE), lambda bi, i: (i, 0, 0)),
                  pl.BlockSpec((1, 2, kb, n2, F_W), lambda bi, i: (bi, 0, i, 0, 0))],
        out_specs=pl.BlockSpec((1, n2, kb * 2 * F_W), lambda bi, i: (bi, 0, i)),
        out_shape=jax.ShapeDtypeStruct((b, n2, n1 * 2 * F_W), BF),
        compiler_params=_params(("parallel", "parallel")),
        name="fft_b",
    )(m2, ct, st, a.reshape(b, 2, n1, n2, F_W))
    return out.reshape(b, l, 2 * F_W)


def _factorizable(l):
    return l % (8 * FFT_N2) == 0


def _position_tables(l):
    return _fft_tables(l) if _factorizable(l) else _dft_tables(l)


def _position_dft(fa_x, tabs):
    return _fft(fa_x, *tabs) if _factorizable(fa_x.shape[1]) else _dft(fa_x, *tabs)


def _gdn_prep_kernel(x_ref, prev_ref, next_ref, ab_ref, cw_ref, alog_ref, dtb_ref,
                     q_ref, k_ref, v_ref, g_ref):
    i = pl.program_id(1)
    last = pl.num_programs(1) - 1
    x = x_ref[0].astype(F32)
    tm = x.shape[0]
    hr = prev_ref.shape[1]
    prev_row = prev_ref[0, hr - 1:hr, :].astype(F32) * (i > 0).astype(F32)
    next_row = next_ref[0, 0:1, :].astype(F32) * (i < last).astype(F32)
    rows = lax.broadcasted_iota(jnp.int32, x.shape, 0)
    x_dn = jnp.where(rows == 0, prev_row, pltpu.roll(x, 1, axis=0))
    x_up = jnp.where(rows == tm - 1, next_row, pltpu.roll(x, tm - 1, axis=0))
    cw = cw_ref[...]
    y = cw[0:1, :] * x_dn + cw[1:2, :] * x + cw[2:3, :] * x_up
    y = y * jax.nn.sigmoid(y)
    for h in range(DN_H):
        qs = y[:, h * DN_DK:(h + 1) * DN_DK]
        ks = y[:, DN_W + h * DN_DK:DN_W + (h + 1) * DN_DK]
        qn = qs * lax.rsqrt(jnp.sum(qs * qs, axis=-1, keepdims=True) + EPS) * (DN_DK ** -0.5)
        kn = ks * lax.rsqrt(jnp.sum(ks * ks, axis=-1, keepdims=True) + EPS)
        q_ref[0, :, h * DN_DK:(h + 1) * DN_DK] = qn.astype(q_ref.dtype)
        k_ref[0, :, h * DN_DK:(h + 1) * DN_DK] = kn.astype(k_ref.dtype)
    v_ref[0] = y[:, 2 * DN_W:].astype(v_ref.dtype)
    a = ab_ref[0]
    cols = lax.broadcasted_iota(jnp.int32, a.shape, 1)
    z = a + dtb_ref[...]
    softplus = jnp.maximum(z, 0.0) + jnp.log1p(jnp.exp(-jnp.abs(z)))
    g = -jnp.exp(alog_ref[...]) * softplus
    g_ref[0] = jnp.where(cols < 2 * DN_H, g, jnp.where(cols < 4 * DN_H, jax.nn.sigmoid(a), 0.0))


def _gdn_prep(qkv, ab, conv_w, a_log, dt_bias):
    b, l, w = qkv.shape
    tm = _tile(l, 512)
    hr = 16
    nb = tm // hr
    pad = jnp.zeros((1, LANE - 2 * DN_H), F32)
    alog = jnp.concatenate([a_log.reshape(1, 2 * DN_H).astype(F32), pad], axis=1)
    dtb = jnp.concatenate([dt_bias.reshape(1, 2 * DN_H).astype(F32), pad], axis=1)
    last_blk = l // hr - 1
    outs = pl.pallas_call(
        _gdn_prep_kernel,
        grid=(b, l // tm),
        in_specs=[pl.BlockSpec((1, tm, w), lambda bi, i: (bi, i, 0)),
                  pl.BlockSpec((1, hr, w), lambda bi, i: (bi, jnp.maximum(i * nb - 1, 0), 0)),
                  pl.BlockSpec((1, hr, w), lambda bi, i: (bi, jnp.minimum((i + 1) * nb, last_blk), 0)),
                  pl.BlockSpec((1, tm, LANE), lambda bi, i: (bi, i, 0)),
                  pl.BlockSpec((3, w), lambda bi, i: (0, 0)),
                  pl.BlockSpec((1, LANE), lambda bi, i: (0, 0)),
                  pl.BlockSpec((1, LANE), lambda bi, i: (0, 0))],
        out_specs=[pl.BlockSpec((1, tm, DN_W), lambda bi, i: (bi, i, 0)),
                   pl.BlockSpec((1, tm, DN_W), lambda bi, i: (bi, i, 0)),
                   pl.BlockSpec((1, tm, DN_W), lambda bi, i: (bi, i, 0)),
                   pl.BlockSpec((1, tm, LANE), lambda bi, i: (bi, i, 0))],
        out_shape=[jax.ShapeDtypeStruct((b, l, DN_W), BF)] * 3 + [jax.ShapeDtypeStruct((b, l, LANE), F32)],
        compiler_params=_params(("parallel", "parallel")),
        name="gdn_prep",
    )(qkv, qkv, qkv, ab, conv_w.astype(F32), alog, dtb)
    return outs


def _nt_dot(a, b):
    return lax.dot_general(a, b, (((1,), (1,)), ((), ())), preferred_element_type=F32)


def _pair_rows(top, bot):
    z = jnp.zeros_like(top)
    return jnp.concatenate([jnp.concatenate([top, z], axis=1), jnp.concatenate([z, bot], axis=1)], axis=0)


def _gdn_local_kernel(q_ref, k_ref, v_ref, g_ref, a1_ref, a2_ref, u_ref, egl_ref, *, cg):
    c = DN_CHUNK
    sq_r = lax.broadcasted_iota(jnp.int32, (c, c), 0)
    sq_c = lax.broadcasted_iota(jnp.int32, (c, c), 1)
    tri = ((sq_r >= sq_c).astype(F32), (sq_r <= sq_c).astype(F32))
    ri = lax.broadcasted_iota(jnp.int32, (c, 2 * c), 0)
    lane = lax.broadcasted_iota(jnp.int32, (c, 2 * c), 1)
    cj = lane & (c - 1)
    first = lane < c
    wide_first = lax.broadcasted_iota(jnp.int32, (c, DN_PW), 1) < DN_DK
    eye = (ri == cj).astype(F32)
    lvl_masks = []
    bsz = 1
    while bsz < c:
        lvl_masks.append((ri // (2 * bsz) == cj // (2 * bsz)) & (ri // bsz != cj // bsz))
        bsz *= 2
    incl = (ri >= cj, ri <= cj)
    strict = (ri > cj, ri < cj)
    last = (c - 1, 0)
    bd_r = lax.broadcasted_iota(jnp.int32, (2 * c, 2 * c), 0) // c
    bd_c = lax.broadcasted_iota(jnp.int32, (2 * c, 2 * c), 1) // c
    bd = bd_r == bd_c

    def bdiag(x):
        return jnp.where(bd, jnp.concatenate([x, x], axis=0), 0.0)

    def spread(x, ca, cb, sel):
        return jnp.where(sel, x[:, ca:ca + 1], x[:, cb:cb + 1])

    systems = []
    for cc in range(cg):
        rows = slice(cc * c, (cc + 1) * c)
        ga = g_ref[0, rows, :]
        pairs = []
        for p in range(DN_PAIRS):
            cols = slice(p * DN_PW, (p + 1) * DN_PW)
            q2, k2, v2 = q_ref[0, rows, cols], k_ref[0, rows, cols], v_ref[0, rows, cols]
            bdk = _pair_rows(k2[:, :DN_DK], k2[:, DN_DK:])
            pairs.append((q2, k2, v2, _nt_dot(k2, bdk), _nt_dot(q2, bdk)))
        for d in range(2):
            gcum = jnp.dot(tri[d], ga, precision=HIGHEST, preferred_element_type=F32)
            gcum_t = jnp.concatenate([gcum, gcum], axis=0).T
            glast = gcum[last[d]:last[d] + 1, :]
            for p in range(DN_PAIRS):
                q2, k2, v2, kk, qk = pairs[p]
                ca, cb = d * DN_H + 2 * p, d * DN_H + 2 * p + 1
                gc = spread(gcum, ca, cb, first)
                gr = jnp.where(first[:1], gcum_t[ca:ca + 1, :], gcum_t[cb:cb + 1, :])
                bc = spread(ga, 2 * DN_H + ca, 2 * DN_H + cb, first)
                decay = jnp.where(incl[d], jnp.exp(jnp.where(incl[d], gc - gr, 0.0)), 0.0)
                a_m = jnp.where(strict[d], bc * kk * decay, 0.0)
                gcw = spread(gcum, ca, cb, wide_first)
                bcw = spread(ga, 2 * DN_H + ca, 2 * DN_H + cb, wide_first)
                glw = spread(glast, ca, cb, wide_first[:1])
                systems.append(dict(idx=(d, cc, p), q=q2.astype(F32), k=k2.astype(F32), v=v2.astype(F32),
                                    bcw=bcw, egw=jnp.exp(gcw), kdw=jnp.exp(glw - gcw), eglw=jnp.exp(glw),
                                    qk=qk * decay, a_m=a_m, t=eye - jnp.where(lvl_masks[0], a_m, 0.0)))
    for lm in lvl_masks[1:]:
        inner = [_dot(jnp.where(lm, s["a_m"], 0.0), bdiag(s["t"])) for s in systems]
        for s, m in zip(systems, inner):
            s["t"] = s["t"] - _dot(s["t"], bdiag(m))
    sols = []
    for s in systems:
        wk = (s["bcw"] * s["egw"]) * s["k"]
        uv = s["bcw"] * s["v"]
        z = jnp.zeros_like(wk[:, :DN_DK])
        rhs = jnp.concatenate([jnp.concatenate([wk[:, :DN_DK], uv[:, :DN_DK], z, z], axis=1),
                               jnp.concatenate([z, z, wk[:, DN_DK:], uv[:, DN_DK:]], axis=1)], axis=0)
        sols.append(_dot(s["t"], rhs))
    for s, sol in zip(systems, sols):
        d, cc, p = s["idx"]
        kd = s["k"] * s["kdw"]
        a1_ref[0, d, cc, p, :c, :] = jnp.concatenate([sol[:, :DN_DK], sol[:, 2 * DN_DK:3 * DN_DK]], axis=1).astype(BF)
        a1_ref[0, d, cc, p, c:, :] = (s["q"] * s["egw"]).astype(BF)
        a2_ref[0, d, cc, p, :c, :] = s["qk"].astype(BF)
        a2_ref[0, d, cc, p, c:, :] = jnp.concatenate([kd[:, :DN_DK], kd[:, DN_DK:]], axis=0).T.astype(BF)
        u_ref[0, d, cc, p] = jnp.concatenate([sol[:, DN_DK:2 * DN_DK], sol[:, 3 * DN_DK:]], axis=1).astype(BF)
        egl_ref[0, d, cc, p:p + 1, :] = s["eglw"]


def _gdn_local(qn, kn, v, gact):
    b, l, _ = qn.shape
    n = l // DN_CHUNK
    cg = 2 if n % 2 == 0 else 1
    c = DN_CHUNK
    row = lambda w: pl.BlockSpec((1, cg * c, w), lambda bi, i: (bi, i, 0))
    out = lambda r, w: pl.BlockSpec((1, 2, cg, DN_PAIRS, r, w), lambda bi, i: (bi, 0, i, 0, 0, 0))
    return pl.pallas_call(
        functools.partial(_gdn_local_kernel, cg=cg),
        grid=(b, n // cg),
        in_specs=[row(DN_W), row(DN_W), row(DN_W), row(LANE)],
        out_specs=[out(2 * c, DN_PW), out(c + DN_DK, 2 * c), out(c, DN_PW),
                   pl.BlockSpec((1, 2, cg, DN_PAIRS, DN_PW), lambda bi, i: (bi, 0, i, 0, 0))],
        out_shape=[jax.ShapeDtypeStruct((b, 2, n, DN_PAIRS, 2 * c, DN_PW), BF),
                   jax.ShapeDtypeStruct((b, 2, n, DN_PAIRS, c + DN_DK, 2 * c), BF),
                   jax.ShapeDtypeStruct((b, 2, n, DN_PAIRS, c, DN_PW), BF),
                   jax.ShapeDtypeStruct((b, 2, n, DN_PAIRS, DN_PW), F32)],
        compiler_params=_params(("parallel", "parallel")),
        name="gdn_local",
    )(qn, kn, v, gact)


def _gdn_scan_kernel(a1f, a2f, uf, ef, a1b, a2b, ub, eb, s0_ref, of_ref, ob_ref, s_ref, *, bg):
    j = pl.program_id(1)

    @pl.when(j == 0)
    def _():
        s_ref[...] = s0_ref[...]

    c = DN_CHUNK
    dirs = ((a1f, a2f, uf, ef, of_ref), (a1b, a2b, ub, eb, ob_ref))
    chains = [(bb, d, p) for bb in range(bg) for d in range(2) for p in range(DN_PAIRS)]
    halves = lambda x: _pair_rows(x[:, :DN_DK], x[:, DN_DK:]).astype(BF)
    st = [s_ref[bb, d, p] for bb, d, p in chains]
    r1 = [jnp.dot(dirs[d][0][bb, 0, 0, p], halves(s), preferred_element_type=F32)
          for (bb, d, p), s in zip(chains, st)]
    un = [dirs[d][2][bb, 0, 0, p].astype(F32) - r[:c] for (bb, d, p), r in zip(chains, r1)]
    r2 = [jnp.dot(dirs[d][1][bb, 0, 0, p], halves(u), preferred_element_type=F32)
          for (bb, d, p), u in zip(chains, un)]
    for (bb, d, p), s, ra, rb in zip(chains, st, r1, r2):
        dirs[d][4][bb, :, p * DN_PW:(p + 1) * DN_PW] = (ra[c:] + rb[:c]).astype(of_ref.dtype)
        s_ref[bb, d, p] = s * dirs[d][3][bb, 0, 0, p:p + 1, :] + rb[c:]


def _gdn_scan(a1, a2, u, egl, s0):
    b, _, n, _, _, _ = a1.shape
    c = DN_CHUNK
    bg = max(g for g in (1, 2, 4) if b % g == 0)
    fwd = lambda bi, j: (bi, 0, j, 0, 0, 0)
    bwd = lambda bi, j: (bi, 1, n - 1 - j, 0, 0, 0)
    blk = lambda a, m: pl.BlockSpec((bg, 1, 1) + a.shape[3:], m)
    eblk = lambda m: pl.BlockSpec((bg, 1, 1, DN_PAIRS, DN_PW), lambda bi, j: m(bi, j)[:5])
    st = pl.BlockSpec((bg, 2, DN_PAIRS, DN_DK, DN_PW), lambda bi, j: (bi, 0, 0, 0, 0))
    return pl.pallas_call(
        functools.partial(_gdn_scan_kernel, bg=bg),
        grid=(b // bg, n),
        in_specs=[blk(a1, fwd), blk(a2, fwd), blk(u, fwd), eblk(fwd),
                  blk(a1, bwd), blk(a2, bwd), blk(u, bwd), eblk(bwd), st],
        out_specs=[pl.BlockSpec((bg, c, DN_W), lambda bi, j: (bi, j, 0)),
                   pl.BlockSpec((bg, c, DN_W), lambda bi, j: (bi, n - 1 - j, 0)), st],
        out_shape=[jax.ShapeDtypeStruct((b, n * c, DN_W), BF), jax.ShapeDtypeStruct((b, n * c, DN_W), BF),
                   jax.ShapeDtypeStruct((b, 2, DN_PAIRS, DN_DK, DN_PW), F32)],
        compiler_params=_params(("parallel", "arbitrary")),
        name="gdn_scan",
    )(a1, a2, u, egl, a1, a2, u, egl, s0)


def _mla_prep_kernel(m_ref, qn_ref, kvn_ref, wq_ref, wqr_ref, wk_ref, wv_ref, cos_ref, sin_ref,
                     q_ref, k_ref, v_ref):
    m = m_ref[0].astype(F32)
    cq = m[:, :M_QL]
    ckv = m[:, M_QL:M_QL + M_KVL]
    kpa = m[:, M_QL + M_KVL:M_QL + M_KVL + LANE]
    kpb = m[:, M_QL + M_KVL + LANE:]
    hq = (cq * lax.rsqrt(jnp.mean(cq * cq, axis=-1, keepdims=True) + EPS) * qn_ref[...]).astype(BF)
    hkv = (ckv * lax.rsqrt(jnp.mean(ckv * ckv, axis=-1, keepdims=True) + EPS) * kvn_ref[...]).astype(BF)
    cos = cos_ref[...]
    sin = sin_ref[...]
    scale = (M_NOPE + M_ROPE) ** -0.5 * math.log2(math.e)
    q_all = jnp.dot(hq, wq_ref[...], preferred_element_type=F32)
    q_rot = jnp.dot(hq, wqr_ref[...], preferred_element_type=F32)
    k_all = jnp.dot(hkv, wk_ref[...], preferred_element_type=F32)
    v_all_t = _nt_dot(wv_ref[...], hkv)
    kpe = kpa * cos + kpb * sin
    rows = lax.broadcasted_iota(jnp.int32, (M_HP, hkv.shape[0]), 0)
    one_row = jnp.where(rows == M_V, 1.0, 0.0)
    for h in range(M_H):
        sl = slice(h * M_HP, (h + 1) * M_HP)
        q_ref[0, h] = ((q_all[:, sl] * cos + q_rot[:, sl] * sin) * scale).astype(q_ref.dtype)
        k_ref[0, h] = (k_all[:, sl] + kpe).astype(k_ref.dtype)
        v_ref[0, h] = (v_all_t[sl, :] + one_row).astype(v_ref.dtype)


def _mla_prep(mla_in, q_norm, kv_norm, wq, wqr, wk, wv, cos_t, sin_t):
    b, l, w = mla_in.shape
    tm = _tile(l, 512)
    full = lambda a: pl.BlockSpec(a.shape, lambda bi, i: (0,) * a.ndim)
    wvt = wv.T
    return pl.pallas_call(
        _mla_prep_kernel,
        grid=(b, l // tm),
        in_specs=[pl.BlockSpec((1, tm, w), lambda bi, i: (bi, i, 0)),
                  pl.BlockSpec((1, M_QL), lambda bi, i: (0, 0)),
                  pl.BlockSpec((1, M_KVL), lambda bi, i: (0, 0)),
                  full(wq), full(wqr), full(wk), full(wvt),
                  pl.BlockSpec((tm, LANE), lambda bi, i: (i, 0)),
                  pl.BlockSpec((tm, LANE), lambda bi, i: (i, 0))],
        out_specs=[pl.BlockSpec((1, M_H, tm, M_HP), lambda bi, i: (bi, 0, i, 0))] * 2
                  + [pl.BlockSpec((1, M_H, M_HP, tm), lambda bi, i: (bi, 0, 0, i))],
        out_shape=[jax.ShapeDtypeStruct((b, M_H, l, M_HP), BF)] * 2
                  + [jax.ShapeDtypeStruct((b, M_H, M_HP, l), BF)],
        compiler_params=_params(("parallel", "parallel")),
        name="mla_prep",
    )(mla_in, q_norm.reshape(1, M_QL).astype(F32), kv_norm.reshape(1, M_KVL).astype(F32),
      wq, wqr, wk, wvt, cos_t, sin_t)


def _attn_kernel(q_ref, *refs, n_src):
    k_refs, v_refs, o_ref = refs[:n_src], refs[n_src:2 * n_src], refs[2 * n_src]
    tq = q_ref.shape[2]
    sub = min(tq, ATT_SUB)
    units = [(hh, r0) for hh in range(2) for r0 in range(0, tq, sub)]

    tiles = [(src, k0, min(ATT_KT, k_r.shape[2] - k0))
             for src, k_r in enumerate(k_refs) for k0 in range(0, k_r.shape[2], ATT_KT)]

    def score_tile(u, tile):
        (hh, r0), (src, k0, ksz) = u, tile
        return _nt_dot(k_refs[src][0, hh, k0:k0 + ksz, :], q_ref[0, hh, r0:r0 + sub, :])

    def value_tile(u, tile, s_t, m):
        (hh, _), (src, k0, ksz) = u, tile
        return jnp.dot(v_refs[src][0, hh, :ATT_VROWS, k0:k0 + ksz], jnp.exp2(s_t - m).astype(BF),
                       preferred_element_type=F32)

    s_cur = [score_tile(units[0], t) for t in tiles]
    for i, u in enumerate(units):
        s_nxt = [score_tile(units[i + 1], t) for t in tiles] if i + 1 < len(units) else None
        m = s_cur[0].max(axis=0, keepdims=True)
        for s_t in s_cur[1:]:
            m = jnp.maximum(m, s_t.max(axis=0, keepdims=True))
        acc = None
        for t, s_t in zip(tiles, s_cur):
            part = value_tile(u, t, s_t, m)
            acc = part if acc is None else acc + part
        hh, r0 = u
        o = acc[:M_V] / acc[M_V:M_V + 1]
        o_ref[0, r0:r0 + sub, hh * M_V:(hh + 1) * M_V] = o.T.astype(o_ref.dtype)
        s_cur = s_nxt


def _attention(q, ks, vs):
    b, _, l, _ = q.shape
    tq = _tile(l, 2 * ATT_SUB)
    n_src = len(ks)
    kv_spec = lambda a: pl.BlockSpec((1, 2) + a.shape[2:], lambda bi, hp, i: (bi, hp, 0, 0))
    return pl.pallas_call(
        functools.partial(_attn_kernel, n_src=n_src),
        grid=(b, M_H // 2, l // tq),
        in_specs=[pl.BlockSpec((1, 2, tq, M_HP), lambda bi, hp, i: (bi, hp, i, 0))]
                 + [kv_spec(a) for a in ks] + [kv_spec(a) for a in vs],
        out_specs=pl.BlockSpec((1, tq, 2 * M_V), lambda bi, hp, i: (bi, i, hp)),
        out_shape=jax.ShapeDtypeStruct((b, l, M_H * M_V), BF),
        compiler_params=_params(("parallel", "parallel", "arbitrary")),
        name="attention",
    )(q, *ks, *vs)


def _merge_kernel(pq_ref, faz_ref, of_ref, ob_ref, dnz_ref, om_ref, mz_ref, gl_ref, x_ref, gm_ref, gp_ref,
                  fw_ref, cc_ref, sc_ref, dnn_ref, wb_ref, wo_ref, o_ref):
    pq = pq_ref[0]
    faz = _silu(faz_ref[0]).astype(F32)
    ya = []
    for g in range(F_GROUPS):
        sl = slice(g * F_GD, (g + 1) * F_GD)
        spec = (jnp.dot(pq[:, sl], cc_ref[...], preferred_element_type=F32)
                - jnp.dot(pq[:, F_W + g * F_GD:F_W + (g + 1) * F_GD], sc_ref[...], preferred_element_type=F32))
        t = jnp.dot(spec.astype(BF), fw_ref[g], preferred_element_type=F32)
        ya.append((t * faz[:, sl]).astype(BF))
    ya = jnp.concatenate(ya, axis=1)
    osum = of_ref[0].astype(F32) + ob_ref[0].astype(F32)
    dnz = _silu(dnz_ref[0]).astype(F32)
    yb = []
    for h in range(DN_H):
        sl = slice(h * DN_DK, (h + 1) * DN_DK)
        seg = osum[:, sl]
        nrm = seg * lax.rsqrt(jnp.mean(seg * seg, axis=-1, keepdims=True) + EPS) * dnn_ref[...]
        yb.append((nrm * dnz[:, sl]).astype(BF))
    yb = jnp.concatenate(yb, axis=1)
    yc = om_ref[0] * _silu(mz_ref[0])
    merged = None
    for idx, yy in enumerate((ya, yb, yc)):
        gate = _sigmoid(gl_ref[0, :, idx * D:(idx + 1) * D]).astype(F32)
        term = gate * jnp.dot(yy, wb_ref[idx], preferred_element_type=F32)
        merged = term if merged is None else merged + term
    y = jnp.dot(merged.astype(BF), wo_ref[...], preferred_element_type=F32)
    yn = y * lax.rsqrt(jnp.mean(y * y, axis=-1, keepdims=True) + EPS) * gp_ref[...]
    o_ref[0] = x_ref[0] + gm_ref[0] * yn


def _merge(pq, fa_z, o_f, o_b, dn_z, o_m, mla_z, gate_logits, x, gate_mod, g_post, fw, cc, sc, dn_norm, wb, wo,
           per_batch_mod):
    b, l, _ = x.shape
    tm = _tile(l, 512)
    row = lambda w: pl.BlockSpec((1, tm, w), lambda bi, i: (bi, i, 0))
    full = lambda a: pl.BlockSpec(a.shape, lambda bi, i: (0,) * a.ndim)
    mod_map = (lambda bi, i: (bi, 0, 0)) if per_batch_mod else (lambda bi, i: (0, 0, 0))
    gp = g_post.reshape(1, D).astype(F32)
    dnn = dn_norm.reshape(1, DN_DK).astype(F32)
    return pl.pallas_call(
        _merge_kernel,
        grid=(b, l // tm),
        in_specs=[row(2 * F_W), row(F_W), row(DN_W), row(DN_W), row(DN_W), row(512), row(512), row(3 * D), row(D),
                  pl.BlockSpec((1, 1, D), mod_map), full(gp), full(fw), full(cc), full(sc), full(dnn),
                  full(wb), full(wo)],
        out_specs=row(D),
        out_shape=jax.ShapeDtypeStruct((b, l, D), F32),
        compiler_params=_params(("parallel", "parallel")),
        name="merge",
    )(pq, fa_z, o_f, o_b, dn_z, o_m, mla_z, gate_logits, x, gate_mod, gp, fw, cc, sc, dnn, wb, wo)


def _rot_cols(w):
    q = M_ROPE // 4
    return jnp.concatenate([-w[..., q:2 * q], w[..., :q], -w[..., 3 * q:], w[..., 2 * q:3 * q]], axis=-1)


def _layer_weights(w_in, mla_w_uq, mla_w_ukv):
    col = lambda n: w_in[:, _OFF[n][0]:_OFF[n][1]]
    zeros = lambda n: jnp.zeros((D, n), F32)
    w_qkv = jnp.concatenate([col("dn_q"), col("dn_k"), col("dn_v")], axis=1)
    w_ab = jnp.concatenate([col("dn_ab"), zeros(LANE - 4 * DN_H)], axis=1)
    kpe = col("kpe")
    rest = LANE - M_NOPE - M_ROPE
    kpe_a = jnp.concatenate([zeros(M_NOPE), kpe, zeros(rest)], axis=1)
    kpe_b = jnp.concatenate([zeros(M_NOPE), _rot_cols(kpe), zeros(rest)], axis=1)
    w_mla = jnp.concatenate([col("cq"), col("ckv"), kpe_a, kpe_b], axis=1)
    grp1 = [col("fa_x"), col("fa_z"), w_qkv, col("dn_z"), w_ab]
    grp2 = [w_mla, col("mla_z"), col("gate")]
    uq = mla_w_uq.reshape(M_QL, M_H, M_NOPE + M_ROPE)
    zq = jnp.zeros((M_QL, M_H, rest), F32)
    wq = jnp.concatenate([uq, zq], axis=-1).reshape(M_QL, M_H * M_HP)
    wqr = jnp.concatenate([jnp.zeros((M_QL, M_H, M_NOPE), F32), _rot_cols(uq[..., M_NOPE:]), zq],
                          axis=-1).reshape(M_QL, M_H * M_HP)
    ukv = mla_w_ukv.reshape(M_KVL, M_H, M_NOPE + M_V)
    zk = jnp.zeros((M_KVL, M_H, M_HP - M_NOPE), F32)
    wk = jnp.concatenate([ukv[..., :M_NOPE], zk], axis=-1).reshape(M_KVL, M_H * M_HP)
    wv = jnp.concatenate([ukv[..., M_NOPE:], jnp.zeros((M_KVL, M_H, M_HP - M_V), F32)],
                         axis=-1).reshape(M_KVL, M_H * M_HP)
    cast = lambda ws: [w.astype(BF) for w in ws]
    return cast(grp1), cast(grp2), cast([wq, wqr, wk, wv])


def _rope_tables(l, rotary):
    ones = jnp.ones((l, M_NOPE), F32)
    pad = jnp.zeros((l, LANE - M_NOPE - M_ROPE), F32)
    if not rotary:
        return (jnp.concatenate([ones, jnp.ones((l, M_ROPE), F32), pad], axis=1),
                jnp.zeros((l, LANE), F32))
    pos = jnp.arange(l, dtype=jnp.int32)
    row = (pos // GRID_W).astype(F32)
    colp = (pos % GRID_W).astype(F32)
    n_freq = M_ROPE // 4
    inv = ROPE_BASE ** (-jnp.arange(n_freq, dtype=F32) / n_freq)
    ang_r = row[:, None] * inv
    ang_c = colp[:, None] * inv
    ang = jnp.concatenate([ang_r, ang_r, ang_c, ang_c], axis=-1)
    return (jnp.concatenate([ones, jnp.cos(ang), pad], axis=1),
            jnp.concatenate([jnp.zeros((l, M_NOPE), F32), jnp.sin(ang), pad], axis=1))


def _chan_tables():
    j = jnp.arange(F_GD, dtype=jnp.int32)
    a = ((j[:, None] * j[None, :]) % F_GD).astype(F32) * (2.0 * math.pi / F_GD)
    nrm = F_GD ** -0.5
    return (jnp.cos(a) * nrm).astype(BF), (jnp.sin(a) * nrm).astype(BF)


def _layer(x, ctx, mod, p, tables, need_ctx_out):
    b = x.shape[0]
    shift, scale, gate = (mod[:b, None, i * D:(i + 1) * D] for i in range(3))
    shift_c, scale_c, gate_c = (mod[b:b + 1, None, i * D:(i + 1) * D] for i in range(3))
    grp1, grp2, (wq, wqr, wk, wv) = _layer_weights(p["w_in"], p["mla_w_uq"], p["mla_w_ukv"])
    dts1 = [BF, BF, BF, BF, F32]
    dts2 = [BF, BF, BF]

    def project(t, sc, sh, per_batch):
        a = _inproj(t, sc, sh, p["g_pre"], grp1, dts1, per_batch)
        bb = _inproj(t, sc, sh, p["g_pre"], grp2, dts2, per_batch)
        return a, bb

    (fa_x, fa_z, qkv, dn_z, ab), (mla_in, mla_z, gl) = project(x, scale, shift, True)
    (fa_xc, fa_zc, qkvc, dn_zc, abc), (mla_inc, mla_zc, glc) = project(ctx, scale_c, shift_c, False)

    qn, kn, vv, gact = _gdn_prep(qkv, ab, p["dn_conv"], p["dn_a_log"], p["dn_dt_bias"])
    qnc, knc, vvc, gactc = _gdn_prep(qkvc, abc, p["dn_conv"], p["dn_a_log"], p["dn_dt_bias"])
    s0 = jnp.zeros((b, 2, DN_PAIRS, DN_DK, DN_PW), F32)
    ofc, obc, s_ctx = _gdn_scan(*_gdn_local(qnc, knc, vvc, gactc), s0)
    of, ob, _ = _gdn_scan(*_gdn_local(qn, kn, vv, gact), s_ctx)

    q_m, k_m, v_m = _mla_prep(mla_in, p["mla_q_norm"], p["mla_kv_norm"], wq, wqr, wk, wv, *tables["rope"])
    qc_m, kc_m, vc_m = _mla_prep(mla_inc, p["mla_q_norm"], p["mla_kv_norm"], wq, wqr, wk, wv, *tables["rope_c"])
    o_m = _attention(q_m, [k_m, kc_m], [v_m, vc_m])

    pq = _position_dft(fa_x, tables["dft"])
    fw = p["f_w"].astype(BF)
    wb = p["w_branch"].astype(BF)
    wo = p["w_out"].astype(BF)
    cc, sc = tables["chan"]
    x_new = _merge(pq, fa_z, of, ob, dn_z, o_m, mla_z, gl, x, gate, p["g_post"], fw, cc, sc, p["dn_norm"],
                   wb, wo, True)
    ctx_new = ctx
    if need_ctx_out:
        pqc = _position_dft(fa_xc, tables["dft_c"])
        oc_m = _attention(qc_m, [kc_m], [vc_m])
        ctx_new = _merge(pqc, fa_zc, ofc, obc, dn_zc, oc_m, mla_zc, glc, ctx, gate_c, p["g_post"], fw, cc, sc,
                         p["dn_norm"], wb, wo, False)
    return x_new, ctx_new


def kernel(x, c, ctx, c_ctx, w_mod, b_mod, g_pre, g_post, w_in, f_w, dn_conv, dn_a_log, dn_dt_bias, dn_norm,
           mla_q_norm, mla_w_uq, mla_kv_norm, mla_w_ukv, w_branch, w_out):
    b, l, _ = x.shape
    lc = ctx.shape[1]
    depth = w_mod.shape[0]
    rows = -(-(b + 1) // 8) * 8
    c_all = jnp.concatenate([c, c_ctx[None, :], jnp.zeros((rows - b - 1, D), F32)], axis=0)
    tables = {"dft": _position_tables(l), "dft_c": _position_tables(lc), "chan": _chan_tables(),
              "rope": _rope_tables(l, True), "rope_c": _rope_tables(lc, False)}
    per_layer = dict(g_pre=g_pre, g_post=g_post, w_in=w_in, f_w=f_w, dn_conv=dn_conv, dn_a_log=dn_a_log,
                     dn_dt_bias=dn_dt_bias, dn_norm=dn_norm, mla_q_norm=mla_q_norm, mla_w_uq=mla_w_uq,
                     mla_kv_norm=mla_kv_norm, mla_w_ukv=mla_w_ukv, w_branch=w_branch, w_out=w_out)
    for li in range(depth):
        p = {k: v[li] for k, v in per_layer.items()}
        mod = _modulation(c_all, w_mod[li], b_mod[li])
        x, ctx = _layer(x, ctx, mod, p, tables, need_ctx_out=(li < depth - 1))
    return x
```

```python
import functools
import math

import jax
import jax.numpy as jnp
from jax import lax
from jax.experimental import pallas as pl
from jax.experimental.pallas import tpu as pltpu

F32 = jnp.float32
BF = jnp.bfloat16
HIGHEST = lax.Precision.HIGHEST

D = 1024
EPS = 1e-6
GRID_W = 64
F_GROUPS = 4
F_GD = 128
F_W = 512
DN_H = 4
DN_DK = 128
DN_W = 512
DN_CHUNK = 64
DN_PAIRS = DN_H // 2
DN_PW = 2 * DN_DK
M_H = 8
M_QL = 384
M_KVL = 256
M_NOPE = 64
M_ROPE = 32
M_V = 64
M_HP = 128
ROPE_BASE = 10000.0
LANE = 128
VMEM_LIMIT = 56 * 1024 * 1024
ATT_SUB = 512
ATT_UNITS = 8
ATT_KT = 4096
ATT_VROWS = 80

_OFF = {}
_o = 0
for _n, _w in (("fa_x", 512), ("fa_z", 512), ("dn_q", 512), ("dn_k", 512), ("dn_v", 512), ("dn_z", 512),
               ("dn_ab", 16), ("cq", M_QL), ("ckv", M_KVL), ("kpe", M_ROPE), ("mla_z", 512), ("gate", 3 * D)):
    _OFF[_n] = (_o, _o + _w)
    _o += _w


def _tile(n, pref, mult=16):
    t = min(n, pref)
    while t > mult and (n % t or t % mult):
        t -= mult
    assert n % t == 0, (n, pref)
    return t


def _params(sem):
    return pltpu.CompilerParams(dimension_semantics=sem, vmem_limit_bytes=VMEM_LIMIT)


def _sigmoid(v):
    return 0.5 * jnp.tanh(0.5 * v) + 0.5


def _silu(v):
    return v * _sigmoid(v)


def _dot(a, b):
    return jnp.dot(a.astype(BF), b.astype(BF), preferred_element_type=F32)


def _mod_kernel(c_ref, w_ref, b_ref, o_ref):
    c = c_ref[...]
    o_ref[...] = jnp.dot(_silu(c), w_ref[...], precision=HIGHEST, preferred_element_type=F32) + b_ref[...]


def _modulation(c_all, w_mod, b_mod):
    r = c_all.shape[0]
    tn = 512
    return pl.pallas_call(
        _mod_kernel,
        grid=(3 * D // tn,),
        in_specs=[pl.BlockSpec((r, D), lambda n: (0, 0)),
                  pl.BlockSpec((D, tn), lambda n: (0, n)),
                  pl.BlockSpec((1, tn), lambda n: (0, n))],
        out_specs=pl.BlockSpec((r, tn), lambda n: (0, n)),
        out_shape=jax.ShapeDtypeStruct((r, 3 * D), F32),
        compiler_params=_params(("arbitrary",)),
        name="modulation",
    )(c_all, w_mod, b_mod.reshape(1, 3 * D))


def _inproj_kernel(x_ref, sc_ref, sh_ref, g_ref, *refs, n_w):
    w_refs, o_refs = refs[:n_w], refs[n_w:]
    x = x_ref[0]
    y = x * lax.rsqrt(jnp.mean(x * x, axis=-1, keepdims=True) + EPS) * g_ref[...]
    hb = (y * (1.0 + sc_ref[0]) + sh_ref[0]).astype(BF)
    for w_ref, o_ref in zip(w_refs, o_refs):
        n = w_ref.shape[1]
        step = 512 if n % 512 == 0 else n
        for c0 in range(0, n, step):
            o_ref[0, :, c0:c0 + step] = jnp.dot(
                hb, w_ref[:, c0:c0 + step], preferred_element_type=F32).astype(o_ref.dtype)


def _inproj(x, scale, shift, g_pre, weights, out_dtypes, per_batch_mod):
    b, l, _ = x.shape
    tm = _tile(l, 1024)
    mod_map = (lambda bi, i: (bi, 0, 0)) if per_batch_mod else (lambda bi, i: (0, 0, 0))
    in_specs = [pl.BlockSpec((1, tm, D), lambda bi, i: (bi, i, 0)),
                pl.BlockSpec((1, 1, D), mod_map),
                pl.BlockSpec((1, 1, D), mod_map),
                pl.BlockSpec((1, D), lambda bi, i: (0, 0))]
    in_specs += [pl.BlockSpec(w.shape, lambda bi, i: (0, 0)) for w in weights]
    out_specs = [pl.BlockSpec((1, tm, w.shape[1]), lambda bi, i: (bi, i, 0)) for w in weights]
    out_shape = [jax.ShapeDtypeStruct((b, l, w.shape[1]), dt) for w, dt in zip(weights, out_dtypes)]
    return pl.pallas_call(
        functools.partial(_inproj_kernel, n_w=len(weights)),
        grid=(b, l // tm),
        in_specs=in_specs, out_specs=out_specs, out_shape=out_shape,
        compiler_params=_params(("parallel", "parallel")),
        name="inproj",
    )(x, scale, shift, g_pre.reshape(1, D), *weights)


def _dft_kernel(c_ref, s_ref, x_ref, o_ref, accp, accq):
    k = pl.program_id(2)

    @pl.when(k == 0)
    def _():
        accp[...] = jnp.zeros_like(accp)
        accq[...] = jnp.zeros_like(accq)

    xk = x_ref[0]
    accp[...] += jnp.dot(c_ref[...], xk, preferred_element_type=F32)
    accq[...] += jnp.dot(s_ref[...], xk, preferred_element_type=F32)

    @pl.when(k == pl.num_programs(2) - 1)
    def _():
        o_ref[0, :, :F_W] = accp[...].astype(o_ref.dtype)
        o_ref[0, :, F_W:] = accq[...].astype(o_ref.dtype)


def _dft_tables(l):
    blk = min(l, 64)
    j1 = jnp.arange(l // blk, dtype=jnp.int32)[:, None] * blk
    j2 = jnp.arange(blk, dtype=jnp.int32)[:, None]
    k = jnp.arange(l, dtype=jnp.int32)[None, :]
    w = 2.0 * math.pi / l
    a = ((j1 * k) % l).astype(F32) * w
    bb = ((j2 * k) % l).astype(F32) * w
    ca, sa, cb, sb = jnp.cos(a), jnp.sin(a), jnp.cos(bb), jnp.sin(bb)
    nrm = l ** -0.5
    cos = (ca[:, None, :] * cb[None] - sa[:, None, :] * sb[None]).reshape(l, l) * nrm
    sin = (sa[:, None, :] * cb[None] + ca[:, None, :] * sb[None]).reshape(l, l) * nrm
    return cos.astype(BF), sin.astype(BF)


def _dft(fa_x, cos_t, sin_t):
    b, l, _ = fa_x.shape
    tm = _tile(l, 1024)
    tk = _tile(l, 1024)
    return pl.pallas_call(
        _dft_kernel,
        grid=(b, l // tm, l // tk),
        in_specs=[pl.BlockSpec((tm, tk), lambda bi, i, k: (i, k)),
                  pl.BlockSpec((tm, tk), lambda bi, i, k: (i, k)),
                  pl.BlockSpec((1, tk, F_W), lambda bi, i, k: (bi, k, 0))],
        out_specs=pl.BlockSpec((1, tm, 2 * F_W), lambda bi, i, k: (bi, i, 0)),
        out_shape=jax.ShapeDtypeStruct((b, l, 2 * F_W), BF),
        scratch_shapes=[pltpu.VMEM((tm, F_W), F32), pltpu.VMEM((tm, F_W), F32)],
        compiler_params=_params(("parallel", "parallel", "arbitrary")),
        name="dft",
    )(cos_t, sin_t, fa_x)


FFT_N2 = 64


def _fft_a_kernel(m_ref, x_ref, o_ref):
    n1 = x_ref.shape[1]
    r = jnp.dot(m_ref[...], x_ref[0], preferred_element_type=F32)
    o_ref[0, 0] = r[:n1].astype(o_ref.dtype)
    o_ref[0, 1] = r[n1:].astype(o_ref.dtype)


def _fft_b_kernel(m_ref, ct_ref, st_ref, a_ref, o_ref):
    kb = a_ref.shape[2]
    reps = F_W // LANE
    for j in range(kb):
        ar = a_ref[0, 0, j].astype(F32)
        ai = a_ref[0, 1, j].astype(F32)
        ct = jnp.concatenate([ct_ref[j]] * reps, axis=1)
        st = jnp.concatenate([st_ref[j]] * reps, axis=1)
        bri = jnp.concatenate([ar * ct + ai * st, ai * ct - ar * st], axis=0).astype(BF)
        r = jnp.dot(m_ref[...], bri, preferred_element_type=F32)
        o_ref[0, :, j * 2 * F_W:j * 2 * F_W + F_W] = r[:FFT_N2].astype(o_ref.dtype)
        o_ref[0, :, j * 2 * F_W + F_W:(j + 1) * 2 * F_W] = r[FFT_N2:].astype(o_ref.dtype)


def _fft_tables(l):
    n1, n2 = l // FFT_N2, FFT_N2
    ang = lambda a, bb, n: ((a[:, None] * bb[None, :]) % n).astype(F32) * (2.0 * math.pi / n)
    i1 = jnp.arange(n1, dtype=jnp.int32)
    i2 = jnp.arange(n2, dtype=jnp.int32)
    a1 = ang(i1, i1, n1)
    m1 = jnp.concatenate([jnp.cos(a1), -jnp.sin(a1)], axis=0).astype(BF)
    a2 = ang(i2, i2, n2)
    c2, s2 = jnp.cos(a2) * l ** -0.5, jnp.sin(a2) * l ** -0.5
    m2 = jnp.concatenate([jnp.concatenate([c2, s2], axis=1),
                          jnp.concatenate([s2, -c2], axis=1)], axis=0).astype(BF)
    at = ang(i1, i2, l)
    bc = lambda t: jnp.broadcast_to(t[:, :, None], (n1, n2, LANE))
    return m1, m2, bc(jnp.cos(at)), bc(jnp.sin(at))


def _fft(fa_x, m1, m2, ct, st):
    b, l, _ = fa_x.shape
    n1, n2 = l // FFT_N2, FFT_N2
    wide = n2 * F_W
    tn = _tile(wide, 8192, LANE)
    a = pl.pallas_call(
        _fft_a_kernel,
        grid=(b, wide // tn),
        in_specs=[pl.BlockSpec((2 * n1, n1), lambda bi, i: (0, 0)),
                  pl.BlockSpec((1, n1, tn), lambda bi, i: (bi, 0, i))],
        out_specs=pl.BlockSpec((1, 2, n1, tn), lambda bi, i: (bi, 0, 0, i)),
        out_shape=jax.ShapeDtypeStruct((b, 2, n1, wide), BF),
        compiler_params=_params(("parallel", "parallel")),
        name="fft_a",
    )(m1, fa_x.reshape(b, n1, wide))
    kb = 8
    out = pl.pallas_call(
        _fft_b_kernel,
        grid=(b, n1 // kb),
        in_specs=[pl.BlockSpec((2 * n2, 2 * n2), lambda bi, i: (0, 0)),
                  pl.BlockSpec((kb, n2, LANE), lambda bi, i: (i, 0, 0)),
                  pl.BlockSpec((kb, n2, LANE), lambda bi, i: (i, 0, 0)),
                  pl.BlockSpec((1, 2, kb, n2, F_W), lambda bi, i: (bi, 0, i, 0, 0))],
        out_specs=pl.BlockSpec((1, n2, kb * 2 * F_W), lambda bi, i: (bi, 0, i)),
        out_shape=jax.ShapeDtypeStruct((b, n2, n1 * 2 * F_W), BF),
        compiler_params=_params(("parallel", "parallel")),
        name="fft_b",
    )(m2, ct, st, a.reshape(b, 2, n1, n2, F_W))
    return out.reshape(b, l, 2 * F_W)


def _factorizable(l):
    return l % (8 * FFT_N2) == 0


def _position_tables(l):
    return _fft_tables(l) if _factorizable(l) else _dft_tables(l)


def _position_dft(fa_x, tabs):
    return _fft(fa_x, *tabs) if _factorizable(fa_x.shape[1]) else _dft(fa_x, *tabs)


def _gdn_prep_block(x_ref, prev_ref, next_ref, ab_ref, cw_ref, alog_ref, dtb_ref):
    i = pl.program_id(1)
    last = pl.num_programs(1) - 1
    x = x_ref[0].astype(F32)
    tm = x.shape[0]
    hr = prev_ref.shape[1]
    prev_row = prev_ref[0, hr - 1:hr, :].astype(F32) * (i > 0).astype(F32)
    next_row = next_ref[0, 0:1, :].astype(F32) * (i < last).astype(F32)
    rows = lax.broadcasted_iota(jnp.int32, x.shape, 0)
    x_dn = jnp.where(rows == 0, prev_row, pltpu.roll(x, 1, axis=0))
    x_up = jnp.where(rows == tm - 1, next_row, pltpu.roll(x, tm - 1, axis=0))
    cw = cw_ref[...]
    y = cw[0:1, :] * x_dn + cw[1:2, :] * x + cw[2:3, :] * x_up
    y = y * jax.nn.sigmoid(y)
    qn, kn = [], []
    for h in range(DN_H):
        qs = y[:, h * DN_DK:(h + 1) * DN_DK]
        ks = y[:, DN_W + h * DN_DK:DN_W + (h + 1) * DN_DK]
        qn.append((qs * lax.rsqrt(jnp.sum(qs * qs, axis=-1, keepdims=True) + EPS) * (DN_DK ** -0.5)).astype(BF))
        kn.append((ks * lax.rsqrt(jnp.sum(ks * ks, axis=-1, keepdims=True) + EPS)).astype(BF))
    a = ab_ref[0]
    cols = lax.broadcasted_iota(jnp.int32, a.shape, 1)
    z = a + dtb_ref[...]
    softplus = jnp.maximum(z, 0.0) + jnp.log1p(jnp.exp(-jnp.abs(z)))
    g = -jnp.exp(alog_ref[...]) * softplus
    gates = jnp.where(cols < 2 * DN_H, g, jnp.where(cols < 4 * DN_H, jax.nn.sigmoid(a), 0.0))
    return jnp.concatenate(qn, axis=1), jnp.concatenate(kn, axis=1), y[:, 2 * DN_W:].astype(BF), gates


def _nt_dot(a, b):
    return lax.dot_general(a, b, (((1,), (1,)), ((), ())), preferred_element_type=F32)


def _pair_rows(top, bot):
    z = jnp.zeros_like(top)
    return jnp.concatenate([jnp.concatenate([top, z], axis=1), jnp.concatenate([z, bot], axis=1)], axis=0)


def _gdn_local_kernel(x_ref, prev_ref, next_ref, ab_ref, cw_ref, alog_ref, dtb_ref,
                      a1_ref, a2_ref, u_ref, egl_ref, *, cg):
    c = DN_CHUNK
    sq_r = lax.broadcasted_iota(jnp.int32, (c, c), 0)
    sq_c = lax.broadcasted_iota(jnp.int32, (c, c), 1)
    tri = ((sq_r >= sq_c).astype(F32), (sq_r <= sq_c).astype(F32))
    ri = lax.broadcasted_iota(jnp.int32, (c, 2 * c), 0)
    lane = lax.broadcasted_iota(jnp.int32, (c, 2 * c), 1)
    cj = lane & (c - 1)
    first = lane < c
    wide_first = lax.broadcasted_iota(jnp.int32, (c, DN_PW), 1) < DN_DK
    eye = (ri == cj).astype(F32)
    lvl_masks = []
    bsz = 1
    while bsz < c:
        lvl_masks.append((ri // (2 * bsz) == cj // (2 * bsz)) & (ri // bsz != cj // bsz))
        bsz *= 2
    incl = (ri >= cj, ri <= cj)
    strict = (ri > cj, ri < cj)
    last = (c - 1, 0)
    bd_r = lax.broadcasted_iota(jnp.int32, (2 * c, 2 * c), 0) // c
    bd_c = lax.broadcasted_iota(jnp.int32, (2 * c, 2 * c), 1) // c
    bd = bd_r == bd_c

    def bdiag(x):
        return jnp.where(bd, jnp.concatenate([x, x], axis=0), 0.0)

    def spread(x, ca, cb, sel):
        return jnp.where(sel, x[:, ca:ca + 1], x[:, cb:cb + 1])

    qn, kn, vv, gates = _gdn_prep_block(x_ref, prev_ref, next_ref, ab_ref, cw_ref, alog_ref, dtb_ref)
    systems = []
    for cc in range(cg):
        rows = slice(cc * c, (cc + 1) * c)
        ga = gates[rows, :]
        pairs = []
        for p in range(DN_PAIRS):
            cols = slice(p * DN_PW, (p + 1) * DN_PW)
            q2, k2, v2 = qn[rows, cols], kn[rows, cols], vv[rows, cols]
            bdk = _pair_rows(k2[:, :DN_DK], k2[:, DN_DK:])
            pairs.append((q2, k2, v2, _nt_dot(k2, bdk), _nt_dot(q2, bdk)))
        for d in range(2):
            gcum = jnp.dot(tri[d], ga, precision=HIGHEST, preferred_element_type=F32)
            gcum_t = jnp.concatenate([gcum, gcum], axis=0).T
            glast = gcum[last[d]:last[d] + 1, :]
            for p in range(DN_PAIRS):
                q2, k2, v2, kk, qk = pairs[p]
                ca, cb = d * DN_H + 2 * p, d * DN_H + 2 * p + 1
                gc = spread(gcum, ca, cb, first)
                gr = jnp.where(first[:1], gcum_t[ca:ca + 1, :], gcum_t[cb:cb + 1, :])
                bc = spread(ga, 2 * DN_H + ca, 2 * DN_H + cb, first)
                decay = jnp.where(incl[d], jnp.exp(jnp.where(incl[d], gc - gr, 0.0)), 0.0)
                a_m = jnp.where(strict[d], bc * kk * decay, 0.0)
                gcw = spread(gcum, ca, cb, wide_first)
                bcw = spread(ga, 2 * DN_H + ca, 2 * DN_H + cb, wide_first)
                glw = spread(glast, ca, cb, wide_first[:1])
                systems.append(dict(idx=(d, cc, p), q=q2.astype(F32), k=k2.astype(F32), v=v2.astype(F32),
                                    bcw=bcw, egw=jnp.exp(gcw), kdw=jnp.exp(glw - gcw), eglw=jnp.exp(glw),
                                    qk=qk * decay, a_m=a_m, t=eye - jnp.where(lvl_masks[0], a_m, 0.0)))
    for lm in lvl_masks[1:]:
        inner = [_dot(jnp.where(lm, s["a_m"], 0.0), bdiag(s["t"])) for s in systems]
        for s, m in zip(systems, inner):
            s["t"] = s["t"] - _dot(s["t"], bdiag(m))
    sols = []
    for s in systems:
        wk = (s["bcw"] * s["egw"]) * s["k"]
        uv = s["bcw"] * s["v"]
        z = jnp.zeros_like(wk[:, :DN_DK])
        rhs = jnp.concatenate([jnp.concatenate([wk[:, :DN_DK], uv[:, :DN_DK], z, z], axis=1),
                               jnp.concatenate([z, z, wk[:, DN_DK:], uv[:, DN_DK:]], axis=1)], axis=0)
        sols.append(_dot(s["t"], rhs))
    for s, sol in zip(systems, sols):
        d, cc, p = s["idx"]
        kd = s["k"] * s["kdw"]
        a1_ref[0, d, cc, p, :c, :] = jnp.concatenate([sol[:, :DN_DK], sol[:, 2 * DN_DK:3 * DN_DK]], axis=1).astype(BF)
        a1_ref[0, d, cc, p, c:, :] = (s["q"] * s["egw"]).astype(BF)
        a2_ref[0, d, cc, p, :c, :] = s["qk"].astype(BF)
        a2_ref[0, d, cc, p, c:, :] = jnp.concatenate([kd[:, :DN_DK], kd[:, DN_DK:]], axis=0).T.astype(BF)
        u_ref[0, d, cc, p] = jnp.concatenate([sol[:, DN_DK:2 * DN_DK], sol[:, 3 * DN_DK:]], axis=1).astype(BF)
        egl_ref[0, d, cc, p:p + 1, :] = s["eglw"]


def _gdn_local(qkv, ab, conv_w, a_log, dt_bias):
    b, l, w = qkv.shape
    n = l // DN_CHUNK
    cg = max(g for g in (1, 2, 4, 8) if n % g == 0)
    c = DN_CHUNK
    tm = cg * c
    hr = 16
    nb = tm // hr
    last_blk = l // hr - 1
    pad = jnp.zeros((1, LANE - 2 * DN_H), F32)
    alog = jnp.concatenate([a_log.reshape(1, 2 * DN_H).astype(F32), pad], axis=1)
    dtb = jnp.concatenate([dt_bias.reshape(1, 2 * DN_H).astype(F32), pad], axis=1)
    out = lambda r, w_: pl.BlockSpec((1, 2, cg, DN_PAIRS, r, w_), lambda bi, i: (bi, 0, i, 0, 0, 0))
    return pl.pallas_call(
        functools.partial(_gdn_local_kernel, cg=cg),
        grid=(b, n // cg),
        in_specs=[pl.BlockSpec((1, tm, w), lambda bi, i: (bi, i, 0)),
                  pl.BlockSpec((1, hr, w), lambda bi, i: (bi, jnp.maximum(i * nb - 1, 0), 0)),
                  pl.BlockSpec((1, hr, w), lambda bi, i: (bi, jnp.minimum((i + 1) * nb, last_blk), 0)),
                  pl.BlockSpec((1, tm, LANE), lambda bi, i: (bi, i, 0)),
                  pl.BlockSpec((3, w), lambda bi, i: (0, 0)),
                  pl.BlockSpec((1, LANE), lambda bi, i: (0, 0)),
                  pl.BlockSpec((1, LANE), lambda bi, i: (0, 0))],
        out_specs=[out(2 * c, DN_PW), out(c + DN_DK, 2 * c), out(c, DN_PW),
                   pl.BlockSpec((1, 2, cg, DN_PAIRS, DN_PW), lambda bi, i: (bi, 0, i, 0, 0))],
        out_shape=[jax.ShapeDtypeStruct((b, 2, n, DN_PAIRS, 2 * c, DN_PW), BF),
                   jax.ShapeDtypeStruct((b, 2, n, DN_PAIRS, c + DN_DK, 2 * c), BF),
                   jax.ShapeDtypeStruct((b, 2, n, DN_PAIRS, c, DN_PW), BF),
                   jax.ShapeDtypeStruct((b, 2, n, DN_PAIRS, DN_PW), F32)],
        compiler_params=_params(("parallel", "parallel")),
        name="gdn_local",
    )(qkv, qkv, qkv, ab, conv_w.astype(F32), alog, dtb)


def _gdn_scan_kernel(a1f, a2f, uf, ef, a1b, a2b, ub, eb, s0_ref, of_ref, ob_ref, s_ref, *, bg):
    j = pl.program_id(1)

    @pl.when(j == 0)
    def _():
        s_ref[...] = s0_ref[...]

    c = DN_CHUNK
    dirs = ((a1f, a2f, uf, ef, of_ref), (a1b, a2b, ub, eb, ob_ref))
    chains = [(bb, d, p) for bb in range(bg) for d in range(2) for p in range(DN_PAIRS)]
    halves = lambda x: _pair_rows(x[:, :DN_DK], x[:, DN_DK:]).astype(BF)
    st = [s_ref[bb, d, p] for bb, d, p in chains]
    r1 = [jnp.dot(dirs[d][0][bb, 0, 0, p], halves(s), preferred_element_type=F32)
          for (bb, d, p), s in zip(chains, st)]
    un = [dirs[d][2][bb, 0, 0, p].astype(F32) - r[:c] for (bb, d, p), r in zip(chains, r1)]
    r2 = [jnp.dot(dirs[d][1][bb, 0, 0, p], halves(u), preferred_element_type=F32)
          for (bb, d, p), u in zip(chains, un)]
    for (bb, d, p), s, ra, rb in zip(chains, st, r1, r2):
        dirs[d][4][bb, :, p * DN_PW:(p + 1) * DN_PW] = (ra[c:] + rb[:c]).astype(of_ref.dtype)
        s_ref[bb, d, p] = s * dirs[d][3][bb, 0, 0, p:p + 1, :] + rb[c:]


def _gdn_scan(a1, a2, u, egl, s0):
    b, _, n, _, _, _ = a1.shape
    c = DN_CHUNK
    bg = max(g for g in (1, 2, 4, 8) if b % g == 0)
    fwd = lambda bi, j: (bi, 0, j, 0, 0, 0)
    bwd = lambda bi, j: (bi, 1, n - 1 - j, 0, 0, 0)
    blk = lambda a, m: pl.BlockSpec((bg, 1, 1) + a.shape[3:], m)
    eblk = lambda m: pl.BlockSpec((bg, 1, 1, DN_PAIRS, DN_PW), lambda bi, j: m(bi, j)[:5])
    st = pl.BlockSpec((bg, 2, DN_PAIRS, DN_DK, DN_PW), lambda bi, j: (bi, 0, 0, 0, 0))
    return pl.pallas_call(
        functools.partial(_gdn_scan_kernel, bg=bg),
        grid=(b // bg, n),
        in_specs=[blk(a1, fwd), blk(a2, fwd), blk(u, fwd), eblk(fwd),
                  blk(a1, bwd), blk(a2, bwd), blk(u, bwd), eblk(bwd), st],
        out_specs=[pl.BlockSpec((bg, c, DN_W), lambda bi, j: (bi, j, 0)),
                   pl.BlockSpec((bg, c, DN_W), lambda bi, j: (bi, n - 1 - j, 0)), st],
        out_shape=[jax.ShapeDtypeStruct((b, n * c, DN_W), BF), jax.ShapeDtypeStruct((b, n * c, DN_W), BF),
                   jax.ShapeDtypeStruct((b, 2, DN_PAIRS, DN_DK, DN_PW), F32)],
        compiler_params=_params(("parallel", "arbitrary")),
        name="gdn_scan",
    )(a1, a2, u, egl, a1, a2, u, egl, s0)


def _mla_prep_kernel(m_ref, qn_ref, kvn_ref, wq_ref, wqr_ref, wk_ref, wv_ref, cos_ref, sin_ref,
                     q_ref, k_ref, v_ref):
    m = m_ref[0].astype(F32)
    cq = m[:, :M_QL]
    ckv = m[:, M_QL:M_QL + M_KVL]
    kpa = m[:, M_QL + M_KVL:M_QL + M_KVL + LANE]
    kpb = m[:, M_QL + M_KVL + LANE:]
    hq = (cq * lax.rsqrt(jnp.mean(cq * cq, axis=-1, keepdims=True) + EPS) * qn_ref[...]).astype(BF)
    hkv = (ckv * lax.rsqrt(jnp.mean(ckv * ckv, axis=-1, keepdims=True) + EPS) * kvn_ref[...]).astype(BF)
    cos = cos_ref[...]
    sin = sin_ref[...]
    scale = (M_NOPE + M_ROPE) ** -0.5 * math.log2(math.e)
    q_all = jnp.dot(hq, wq_ref[...], preferred_element_type=F32)
    q_rot = jnp.dot(hq, wqr_ref[...], preferred_element_type=F32)
    k_all = jnp.dot(hkv, wk_ref[...], preferred_element_type=F32)
    v_all_t = _nt_dot(wv_ref[...], hkv)
    kpe = kpa * cos + kpb * sin
    rows = lax.broadcasted_iota(jnp.int32, (M_HP, hkv.shape[0]), 0)
    one_row = jnp.where(rows == M_V, 1.0, 0.0)
    for h in range(M_H):
        sl = slice(h * M_HP, (h + 1) * M_HP)
        q_ref[0, h] = ((q_all[:, sl] * cos + q_rot[:, sl] * sin) * scale).astype(q_ref.dtype)
        k_ref[0, h] = (k_all[:, sl] + kpe).astype(k_ref.dtype)
        v_ref[0, h] = (v_all_t[sl, :] + one_row).astype(v_ref.dtype)


def _mla_prep(mla_in, q_norm, kv_norm, wq, wqr, wk, wv, cos_t, sin_t):
    b, l, w = mla_in.shape
    tm = _tile(l, 512)
    full = lambda a: pl.BlockSpec(a.shape, lambda bi, i: (0,) * a.ndim)
    wvt = wv.T
    return pl.pallas_call(
        _mla_prep_kernel,
        grid=(b, l // tm),
        in_specs=[pl.BlockSpec((1, tm, w), lambda bi, i: (bi, i, 0)),
                  pl.BlockSpec((1, M_QL), lambda bi, i: (0, 0)),
                  pl.BlockSpec((1, M_KVL), lambda bi, i: (0, 0)),
                  full(wq), full(wqr), full(wk), full(wvt),
                  pl.BlockSpec((tm, LANE), lambda bi, i: (i, 0)),
                  pl.BlockSpec((tm, LANE), lambda bi, i: (i, 0))],
        out_specs=[pl.BlockSpec((1, M_H, tm, M_HP), lambda bi, i: (bi, 0, i, 0))] * 2
                  + [pl.BlockSpec((1, M_H, M_HP, tm), lambda bi, i: (bi, 0, 0, i))],
        out_shape=[jax.ShapeDtypeStruct((b, M_H, l, M_HP), BF)] * 2
                  + [jax.ShapeDtypeStruct((b, M_H, M_HP, l), BF)],
        compiler_params=_params(("parallel", "parallel")),
        name="mla_prep",
    )(mla_in, q_norm.reshape(1, M_QL).astype(F32), kv_norm.reshape(1, M_KVL).astype(F32),
      wq, wqr, wk, wvt, cos_t, sin_t)


def _attn_kernel(q_ref, *refs, n_src):
    k_refs, v_refs, o_ref = refs[:n_src], refs[n_src:2 * n_src], refs[2 * n_src]
    tq = q_ref.shape[2]
    sub = min(tq, ATT_SUB)
    units = [(hh, r0) for hh in range(2) for r0 in range(0, tq, sub)]

    tiles = [(src, k0, min(ATT_KT, k_r.shape[2] - k0))
             for src, k_r in enumerate(k_refs) for k0 in range(0, k_r.shape[2], ATT_KT)]

    def score_tile(u, tile):
        (hh, r0), (src, k0, ksz) = u, tile
        return _nt_dot(k_refs[src][0, hh, k0:k0 + ksz, :], q_ref[0, hh, r0:r0 + sub, :])

    def value_tile(u, tile, s_t, m):
        (hh, _), (src, k0, ksz) = u, tile
        return jnp.dot(v_refs[src][0, hh, :ATT_VROWS, k0:k0 + ksz], jnp.exp2(s_t - m).astype(BF),
                       preferred_element_type=F32)

    s_cur = [score_tile(units[0], t) for t in tiles]
    for i, u in enumerate(units):
        s_nxt = [score_tile(units[i + 1], t) for t in tiles] if i + 1 < len(units) else None
        m = s_cur[0].max(axis=0, keepdims=True)
        for s_t in s_cur[1:]:
            m = jnp.maximum(m, s_t.max(axis=0, keepdims=True))
        acc = None
        for t, s_t in zip(tiles, s_cur):
            part = value_tile(u, t, s_t, m)
            acc = part if acc is None else acc + part
        hh, r0 = u
        o = acc[:M_V] / acc[M_V:M_V + 1]
        o_ref[0, r0:r0 + sub, hh * M_V:(hh + 1) * M_V] = o.T.astype(o_ref.dtype)
        s_cur = s_nxt


def _attention(q, ks, vs):
    b, _, l, _ = q.shape
    tq = _tile(l, ATT_UNITS * ATT_SUB)
    n_src = len(ks)
    kv_spec = lambda a: pl.BlockSpec((1, 2) + a.shape[2:], lambda bi, hp, i: (bi, hp, 0, 0))
    return pl.pallas_call(
        functools.partial(_attn_kernel, n_src=n_src),
        grid=(b, M_H // 2, l // tq),
        in_specs=[pl.BlockSpec((1, 2, tq, M_HP), lambda bi, hp, i: (bi, hp, i, 0))]
                 + [kv_spec(a) for a in ks] + [kv_spec(a) for a in vs],
        out_specs=pl.BlockSpec((1, tq, 2 * M_V), lambda bi, hp, i: (bi, i, hp)),
        out_shape=jax.ShapeDtypeStruct((b, l, M_H * M_V), BF),
        compiler_params=_params(("parallel", "parallel", "arbitrary")),
        name="attention",
    )(q, *ks, *vs)


def _merge_kernel(pq_ref, faz_ref, of_ref, ob_ref, dnz_ref, om_ref, mz_ref, gl_ref, x_ref, gm_ref, gp_ref,
                  fw_ref, cc_ref, sc_ref, dnn_ref, wb_ref, wo_ref, o_ref):
    pq = pq_ref[0]
    faz = _silu(faz_ref[0]).astype(F32)
    ya = []
    for g in range(F_GROUPS):
        sl = slice(g * F_GD, (g + 1) * F_GD)
        spec = (jnp.dot(pq[:, sl], cc_ref[...], preferred_element_type=F32)
                - jnp.dot(pq[:, F_W + g * F_GD:F_W + (g + 1) * F_GD], sc_ref[...], preferred_element_type=F32))
        t = jnp.dot(spec.astype(BF), fw_ref[g], preferred_element_type=F32)
        ya.append((t * faz[:, sl]).astype(BF))
    ya = jnp.concatenate(ya, axis=1)
    osum = of_ref[0].astype(F32) + ob_ref[0].astype(F32)
    dnz = _silu(dnz_ref[0]).astype(F32)
    yb = []
    for h in range(DN_H):
        sl = slice(h * DN_DK, (h + 1) * DN_DK)
        seg = osum[:, sl]
        nrm = seg * lax.rsqrt(jnp.mean(seg * seg, axis=-1, keepdims=True) + EPS) * dnn_ref[...]
        yb.append((nrm * dnz[:, sl]).astype(BF))
    yb = jnp.concatenate(yb, axis=1)
    yc = om_ref[0] * _silu(mz_ref[0])
    merged = None
    for idx, yy in enumerate((ya, yb, yc)):
        gate = _sigmoid(gl_ref[0, :, idx * D:(idx + 1) * D]).astype(F32)
        term = gate * jnp.dot(yy, wb_ref[idx], preferred_element_type=F32)
        merged = term if merged is None else merged + term
    y = jnp.dot(merged.astype(BF), wo_ref[...], preferred_element_type=F32)
    yn = y * lax.rsqrt(jnp.mean(y * y, axis=-1, keepdims=True) + EPS) * gp_ref[...]
    o_ref[0] = x_ref[0] + gm_ref[0] * yn


def _merge(pq, fa_z, o_f, o_b, dn_z, o_m, mla_z, gate_logits, x, gate_mod, g_post, fw, cc, sc, dn_norm, wb, wo,
           per_batch_mod):
    b, l, _ = x.shape
    tm = _tile(l, 512)
    row = lambda w: pl.BlockSpec((1, tm, w), lambda bi, i: (bi, i, 0))
    full = lambda a: pl.BlockSpec(a.shape, lambda bi, i: (0,) * a.ndim)
    mod_map = (lambda bi, i: (bi, 0, 0)) if per_batch_mod else (lambda bi, i: (0, 0, 0))
    gp = g_post.reshape(1, D).astype(F32)
    dnn = dn_norm.reshape(1, DN_DK).astype(F32)
    return pl.pallas_call(
        _merge_kernel,
        grid=(b, l // tm),
        in_specs=[row(2 * F_W), row(F_W), row(DN_W), row(DN_W), row(DN_W), row(512), row(512), row(3 * D), row(D),
                  pl.BlockSpec((1, 1, D), mod_map), full(gp), full(fw), full(cc), full(sc), full(dnn),
                  full(wb), full(wo)],
        out_specs=row(D),
        out_shape=jax.ShapeDtypeStruct((b, l, D), F32),
        compiler_params=_params(("parallel", "parallel")),
        name="merge",
    )(pq, fa_z, o_f, o_b, dn_z, o_m, mla_z, gate_logits, x, gate_mod, gp, fw, cc, sc, dnn, wb, wo)


def _rot_cols(w):
    q = M_ROPE // 4
    return jnp.concatenate([-w[..., q:2 * q], w[..., :q], -w[..., 3 * q:], w[..., 2 * q:3 * q]], axis=-1)


def _layer_weights(w_in, mla_w_uq, mla_w_ukv):
    col = lambda n: w_in[:, _OFF[n][0]:_OFF[n][1]]
    zeros = lambda n: jnp.zeros((D, n), F32)
    w_qkv = jnp.concatenate([col("dn_q"), col("dn_k"), col("dn_v")], axis=1)
    w_ab = jnp.concatenate([col("dn_ab"), zeros(LANE - 4 * DN_H)], axis=1)
    kpe = col("kpe")
    rest = LANE - M_NOPE - M_ROPE
    kpe_a = jnp.concatenate([zeros(M_NOPE), kpe, zeros(rest)], axis=1)
    kpe_b = jnp.concatenate([zeros(M_NOPE), _rot_cols(kpe), zeros(rest)], axis=1)
    w_mla = jnp.concatenate([col("cq"), col("ckv"), kpe_a, kpe_b], axis=1)
    grp1 = [col("fa_x"), col("fa_z"), w_qkv, col("dn_z"), w_ab]
    grp2 = [w_mla, col("mla_z"), col("gate")]
    uq = mla_w_uq.reshape(M_QL, M_H, M_NOPE + M_ROPE)
    zq = jnp.zeros((M_QL, M_H, rest), F32)
    wq = jnp.concatenate([uq, zq], axis=-1).reshape(M_QL, M_H * M_HP)
    wqr = jnp.concatenate([jnp.zeros((M_QL, M_H, M_NOPE), F32), _rot_cols(uq[..., M_NOPE:]), zq],
                          axis=-1).reshape(M_QL, M_H * M_HP)
    ukv = mla_w_ukv.reshape(M_KVL, M_H, M_NOPE + M_V)
    zk = jnp.zeros((M_KVL, M_H, M_HP - M_NOPE), F32)
    wk = jnp.concatenate([ukv[..., :M_NOPE], zk], axis=-1).reshape(M_KVL, M_H * M_HP)
    wv = jnp.concatenate([ukv[..., M_NOPE:], jnp.zeros((M_KVL, M_H, M_HP - M_V), F32)],
                         axis=-1).reshape(M_KVL, M_H * M_HP)
    cast = lambda ws: [w.astype(BF) for w in ws]
    return cast(grp1), cast(grp2), cast([wq, wqr, wk, wv])


def _rope_tables(l, rotary):
    ones = jnp.ones((l, M_NOPE), F32)
    pad = jnp.zeros((l, LANE - M_NOPE - M_ROPE), F32)
    if not rotary:
        return (jnp.concatenate([ones, jnp.ones((l, M_ROPE), F32), pad], axis=1),
                jnp.zeros((l, LANE), F32))
    pos = jnp.arange(l, dtype=jnp.int32)
    row = (pos // GRID_W).astype(F32)
    colp = (pos % GRID_W).astype(F32)
    n_freq = M_ROPE // 4
    inv = ROPE_BASE ** (-jnp.arange(n_freq, dtype=F32) / n_freq)
    ang_r = row[:, None] * inv
    ang_c = colp[:, None] * inv
    ang = jnp.concatenate([ang_r, ang_r, ang_c, ang_c], axis=-1)
    return (jnp.concatenate([ones, jnp.cos(ang), pad], axis=1),
            jnp.concatenate([jnp.zeros((l, M_NOPE), F32), jnp.sin(ang), pad], axis=1))


def _chan_tables():
    j = jnp.arange(F_GD, dtype=jnp.int32)
    a = ((j[:, None] * j[None, :]) % F_GD).astype(F32) * (2.0 * math.pi / F_GD)
    nrm = F_GD ** -0.5
    return (jnp.cos(a) * nrm).astype(BF), (jnp.sin(a) * nrm).astype(BF)


def _layer(x, ctx, mod, p, tables, need_ctx_out):
    b = x.shape[0]
    shift, scale, gate = (mod[:b, None, i * D:(i + 1) * D] for i in range(3))
    shift_c, scale_c, gate_c = (mod[b:b + 1, None, i * D:(i + 1) * D] for i in range(3))
    grp1, grp2, (wq, wqr, wk, wv) = _layer_weights(p["w_in"], p["mla_w_uq"], p["mla_w_ukv"])
    dts1 = [BF, BF, BF, BF, F32]
    dts2 = [BF, BF, BF]

    def project(t, sc, sh, per_batch):
        a = _inproj(t, sc, sh, p["g_pre"], grp1, dts1, per_batch)
        bb = _inproj(t, sc, sh, p["g_pre"], grp2, dts2, per_batch)
        return a, bb

    (fa_x, fa_z, qkv, dn_z, ab), (mla_in, mla_z, gl) = project(x, scale, shift, True)
    (fa_xc, fa_zc, qkvc, dn_zc, abc), (mla_inc, mla_zc, glc) = project(ctx, scale_c, shift_c, False)

    gdn_args = (p["dn_conv"], p["dn_a_log"], p["dn_dt_bias"])
    s0 = jnp.zeros((b, 2, DN_PAIRS, DN_DK, DN_PW), F32)
    ofc, obc, s_ctx = _gdn_scan(*_gdn_local(qkvc, abc, *gdn_args), s0)
    of, ob, _ = _gdn_scan(*_gdn_local(qkv, ab, *gdn_args), s_ctx)

    q_m, k_m, v_m = _mla_prep(mla_in, p["mla_q_norm"], p["mla_kv_norm"], wq, wqr, wk, wv, *tables["rope"])
    qc_m, kc_m, vc_m = _mla_prep(mla_inc, p["mla_q_norm"], p["mla_kv_norm"], wq, wqr, wk, wv, *tables["rope_c"])
    o_m = _attention(q_m, [k_m, kc_m], [v_m, vc_m])

    pq = _position_dft(fa_x, tables["dft"])
    fw = p["f_w"].astype(BF)
    wb = p["w_branch"].astype(BF)
    wo = p["w_out"].astype(BF)
    cc, sc = tables["chan"]
    x_new = _merge(pq, fa_z, of, ob, dn_z, o_m, mla_z, gl, x, gate, p["g_post"], fw, cc, sc, p["dn_norm"],
                   wb, wo, True)
    ctx_new = ctx
    if need_ctx_out:
        pqc = _position_dft(fa_xc, tables["dft_c"])
        oc_m = _attention(qc_m, [kc_m], [vc_m])
        ctx_new = _merge(pqc, fa_zc, ofc, obc, dn_zc, oc_m, mla_zc, glc, ctx, gate_c, p["g_post"], fw, cc, sc,
                         p["dn_norm"], wb, wo, False)
    return x_new, ctx_new


def kernel(x, c, ctx, c_ctx, w_mod, b_mod, g_pre, g_post, w_in, f_w, dn_conv, dn_a_log, dn_dt_bias, dn_norm,
           mla_q_norm, mla_w_uq, mla_kv_norm, mla_w_ukv, w_branch, w_out):
    b, l, _ = x.shape
    lc = ctx.shape[1]
    depth = w_mod.shape[0]
    rows = -(-(b + 1) // 8) * 8
    c_all = jnp.concatenate([c, c_ctx[None, :], jnp.zeros((rows - b - 1, D), F32)], axis=0)
    tables = {"dft": _position_tables(l), "dft_c": _position_tables(lc), "chan": _chan_tables(),
              "rope": _rope_tables(l, True), "rope_c": _rope_tables(lc, False)}
    per_layer = dict(g_pre=g_pre, g_post=g_post, w_in=w_in, f_w=f_w, dn_conv=dn_conv, dn_a_log=dn_a_log,
                     dn_dt_bias=dn_dt_bias, dn_norm=dn_norm, mla_q_norm=mla_q_norm, mla_w_uq=mla_w_uq,
                     mla_kv_norm=mla_kv_norm, mla_w_ukv=mla_w_ukv, w_branch=w_branch, w_out=w_out)
    for li in range(depth):
        p = {k: v[li] for k, v in per_layer.items()}
        mod = _modulation(c_all, w_mod[li], b_mod[li])
        x, ctx = _layer(x, ctx, mod, p, tables, need_ctx_out=(li < depth - 1))
    return x
```

```python
import functools
import math

import jax
import jax.numpy as jnp
from jax import lax
from jax.experimental import pallas as pl
from jax.experimental.pallas import tpu as pltpu

F32 = jnp.float32
BF = jnp.bfloat16
HIGHEST = lax.Precision.HIGHEST

D = 1024
EPS = 1e-6
GRID_W = 64
F_GROUPS = 4
F_GD = 128
F_W = 512
DN_H = 4
DN_DK = 128
DN_W = 512
DN_CHUNK = 64
DN_PAIRS = DN_H // 2
DN_PW = 2 * DN_DK
M_H = 8
M_QL = 384
M_KVL = 256
M_NOPE = 64
M_ROPE = 32
M_V = 64
M_HP = 128
ROPE_BASE = 10000.0
LANE = 128
VMEM_LIMIT = 56 * 1024 * 1024
ATT_SUB = 512
ATT_UNITS = 8
ATT_KT = 4096
ATT_VROWS = 80

_OFF = {}
_o = 0
for _n, _w in (("fa_x", 512), ("fa_z", 512), ("dn_q", 512), ("dn_k", 512), ("dn_v", 512), ("dn_z", 512),
               ("dn_ab", 16), ("cq", M_QL), ("ckv", M_KVL), ("kpe", M_ROPE), ("mla_z", 512), ("gate", 3 * D)):
    _OFF[_n] = (_o, _o + _w)
    _o += _w


def _tile(n, pref, mult=16):
    t = min(n, pref)
    while t > mult and (n % t or t % mult):
        t -= mult
    assert n % t == 0, (n, pref)
    return t


def _params(sem):
    return pltpu.CompilerParams(dimension_semantics=sem, vmem_limit_bytes=VMEM_LIMIT)


def _sigmoid(v):
    return 0.5 * jnp.tanh(0.5 * v) + 0.5


def _silu(v):
    return v * _sigmoid(v)


def _dot(a, b):
    return jnp.dot(a.astype(BF), b.astype(BF), preferred_element_type=F32)


def _mod_kernel(c_ref, w_ref, b_ref, o_ref):
    c = c_ref[...]
    o_ref[...] = jnp.dot(_silu(c), w_ref[...], precision=HIGHEST, preferred_element_type=F32) + b_ref[...]


def _modulation(c_all, w_mod, b_mod):
    r = c_all.shape[0]
    tn = 512
    return pl.pallas_call(
        _mod_kernel,
        grid=(3 * D // tn,),
        in_specs=[pl.BlockSpec((r, D), lambda n: (0, 0)),
                  pl.BlockSpec((D, tn), lambda n: (0, n)),
                  pl.BlockSpec((1, tn), lambda n: (0, n))],
        out_specs=pl.BlockSpec((r, tn), lambda n: (0, n)),
        out_shape=jax.ShapeDtypeStruct((r, 3 * D), F32),
        compiler_params=_params(("arbitrary",)),
        name="modulation",
    )(c_all, w_mod, b_mod.reshape(1, 3 * D))


def _inproj_kernel(x_ref, sc_ref, sh_ref, g_ref, *refs, n_w, fold_first):
    w_refs, o_refs = refs[:n_w], refs[n_w:2 * n_w]
    x = x_ref[0]
    y = x * lax.rsqrt(jnp.mean(x * x, axis=-1, keepdims=True) + EPS) * g_ref[...]
    hb = (y * (1.0 + sc_ref[0]) + sh_ref[0]).astype(BF)
    for idx, (w_ref, o_ref) in enumerate(zip(w_refs, o_refs)):
        n = w_ref.shape[1]
        if fold_first and idx == 0:
            scr = refs[2 * n_w]
            res = jnp.dot(hb, w_ref[...], preferred_element_type=F32)
            for j in range(n // LANE):
                scr[j] = res[:, j * LANE:(j + 1) * LANE]
            groups = scr.shape[1] // FFT_N2
            for l2 in range(FFT_N2):
                for j in range(n // LANE):
                    o_ref[0, :, l2 * n + j * LANE:l2 * n + (j + 1) * LANE] = (
                        scr[j, pl.ds(l2, groups, stride=FFT_N2), :].astype(o_ref.dtype))
            continue
        step = 512 if n % 512 == 0 else n
        for c0 in range(0, n, step):
            o_ref[0, :, c0:c0 + step] = jnp.dot(
                hb, w_ref[:, c0:c0 + step], preferred_element_type=F32).astype(o_ref.dtype)


def _inproj(x, scale, shift, g_pre, weights, out_dtypes, per_batch_mod, fold_first=False):
    b, l, _ = x.shape
    tm = _tile(l, 1024)
    mod_map = (lambda bi, i: (bi, 0, 0)) if per_batch_mod else (lambda bi, i: (0, 0, 0))
    in_specs = [pl.BlockSpec((1, tm, D), lambda bi, i: (bi, i, 0)),
                pl.BlockSpec((1, 1, D), mod_map),
                pl.BlockSpec((1, 1, D), mod_map),
                pl.BlockSpec((1, D), lambda bi, i: (0, 0))]
    in_specs += [pl.BlockSpec(w.shape, lambda bi, i: (0, 0)) for w in weights]
    out_specs = [pl.BlockSpec((1, tm, w.shape[1]), lambda bi, i: (bi, i, 0)) for w in weights]
    out_shape = [jax.ShapeDtypeStruct((b, l, w.shape[1]), dt) for w, dt in zip(weights, out_dtypes)]
    scratch = []
    if fold_first:
        n0 = weights[0].shape[1]
        out_specs[0] = pl.BlockSpec((1, tm // FFT_N2, FFT_N2 * n0), lambda bi, i: (bi, i, 0))
        out_shape[0] = jax.ShapeDtypeStruct((b, l // FFT_N2, FFT_N2 * n0), out_dtypes[0])
        scratch = [pltpu.VMEM((n0 // LANE, tm, LANE), F32)]
    return pl.pallas_call(
        functools.partial(_inproj_kernel, n_w=len(weights), fold_first=fold_first),
        grid=(b, l // tm),
        in_specs=in_specs, out_specs=out_specs, out_shape=out_shape, scratch_shapes=scratch,
        compiler_params=_params(("parallel", "parallel")),
        name="inproj",
    )(x, scale, shift, g_pre.reshape(1, D), *weights)


def _dft_kernel(c_ref, s_ref, x_ref, o_ref, accp, accq):
    k = pl.program_id(2)

    @pl.when(k == 0)
    def _():
        accp[...] = jnp.zeros_like(accp)
        accq[...] = jnp.zeros_like(accq)

    xk = x_ref[0]
    accp[...] += jnp.dot(c_ref[...], xk, preferred_element_type=F32)
    accq[...] += jnp.dot(s_ref[...], xk, preferred_element_type=F32)

    @pl.when(k == pl.num_programs(2) - 1)
    def _():
        o_ref[0, :, :F_W] = accp[...].astype(o_ref.dtype)
        o_ref[0, :, F_W:] = accq[...].astype(o_ref.dtype)


def _dft_tables(l):
    blk = min(l, 64)
    j1 = jnp.arange(l // blk, dtype=jnp.int32)[:, None] * blk
    j2 = jnp.arange(blk, dtype=jnp.int32)[:, None]
    k = jnp.arange(l, dtype=jnp.int32)[None, :]
    w = 2.0 * math.pi / l
    a = ((j1 * k) % l).astype(F32) * w
    bb = ((j2 * k) % l).astype(F32) * w
    ca, sa, cb, sb = jnp.cos(a), jnp.sin(a), jnp.cos(bb), jnp.sin(bb)
    nrm = l ** -0.5
    cos = (ca[:, None, :] * cb[None] - sa[:, None, :] * sb[None]).reshape(l, l) * nrm
    sin = (sa[:, None, :] * cb[None] + ca[:, None, :] * sb[None]).reshape(l, l) * nrm
    return cos.astype(BF), sin.astype(BF)


def _dft(fa_x, cos_t, sin_t):
    b, l, _ = fa_x.shape
    tm = _tile(l, 1024)
    tk = _tile(l, 1024)
    return pl.pallas_call(
        _dft_kernel,
        grid=(b, l // tm, l // tk),
        in_specs=[pl.BlockSpec((tm, tk), lambda bi, i, k: (i, k)),
                  pl.BlockSpec((tm, tk), lambda bi, i, k: (i, k)),
                  pl.BlockSpec((1, tk, F_W), lambda bi, i, k: (bi, k, 0))],
        out_specs=pl.BlockSpec((1, tm, 2 * F_W), lambda bi, i, k: (bi, i, 0)),
        out_shape=jax.ShapeDtypeStruct((b, l, 2 * F_W), BF),
        scratch_shapes=[pltpu.VMEM((tm, F_W), F32), pltpu.VMEM((tm, F_W), F32)],
        compiler_params=_params(("parallel", "parallel", "arbitrary")),
        name="dft",
    )(cos_t, sin_t, fa_x)


FFT_N2 = 64


def _fft_a_kernel(m_ref, x_ref, o_ref):
    n1 = x_ref.shape[1]
    r = jnp.dot(m_ref[...], x_ref[0], preferred_element_type=F32)
    o_ref[0, 0] = r[:n1].astype(o_ref.dtype)
    o_ref[0, 1] = r[n1:].astype(o_ref.dtype)


def _fft_b_kernel(m_ref, ct_ref, st_ref, a_ref, o_ref):
    kb = a_ref.shape[2]
    reps = F_W // LANE
    for j in range(kb):
        ar = a_ref[0, 0, j].astype(F32)
        ai = a_ref[0, 1, j].astype(F32)
        ct = jnp.concatenate([ct_ref[j]] * reps, axis=1)
        st = jnp.concatenate([st_ref[j]] * reps, axis=1)
        bri = jnp.concatenate([ar * ct + ai * st, ai * ct - ar * st], axis=0).astype(BF)
        r = jnp.dot(m_ref[...], bri, preferred_element_type=F32)
        o_ref[0, :, j * 2 * F_W:j * 2 * F_W + F_W] = r[:FFT_N2].astype(o_ref.dtype)
        o_ref[0, :, j * 2 * F_W + F_W:(j + 1) * 2 * F_W] = r[FFT_N2:].astype(o_ref.dtype)


def _fft_tables(l):
    n1, n2 = l // FFT_N2, FFT_N2
    ang = lambda a, bb, n: ((a[:, None] * bb[None, :]) % n).astype(F32) * (2.0 * math.pi / n)
    i1 = jnp.arange(n1, dtype=jnp.int32)
    i2 = jnp.arange(n2, dtype=jnp.int32)
    a1 = ang(i1, i1, n1)
    m1 = jnp.concatenate([jnp.cos(a1), -jnp.sin(a1)], axis=0).astype(BF)
    a2 = ang(i2, i2, n2)
    c2, s2 = jnp.cos(a2) * l ** -0.5, jnp.sin(a2) * l ** -0.5
    m2 = jnp.concatenate([jnp.concatenate([c2, s2], axis=1),
                          jnp.concatenate([s2, -c2], axis=1)], axis=0).astype(BF)
    at = ang(i1, i2, l)
    bc = lambda t: jnp.broadcast_to(t[:, :, None], (n1, n2, LANE))
    return m1, m2, bc(jnp.cos(at)), bc(jnp.sin(at))


def _fft(fa_x, m1, m2, ct, st):
    b, n1, wide = fa_x.shape
    n2 = FFT_N2
    l = n1 * n2
    tn = _tile(wide, 8192, LANE)
    a = pl.pallas_call(
        _fft_a_kernel,
        grid=(b, wide // tn),
        in_specs=[pl.BlockSpec((2 * n1, n1), lambda bi, i: (0, 0)),
                  pl.BlockSpec((1, n1, tn), lambda bi, i: (bi, 0, i))],
        out_specs=pl.BlockSpec((1, 2, n1, tn), lambda bi, i: (bi, 0, 0, i)),
        out_shape=jax.ShapeDtypeStruct((b, 2, n1, wide), BF),
        compiler_params=_params(("parallel", "parallel")),
        name="fft_a",
    )(m1, fa_x)
    kb = 8
    out = pl.pallas_call(
        _fft_b_kernel,
        grid=(b, n1 // kb),
        in_specs=[pl.BlockSpec((2 * n2, 2 * n2), lambda bi, i: (0, 0)),
                  pl.BlockSpec((kb, n2, LANE), lambda bi, i: (i, 0, 0)),
                  pl.BlockSpec((kb, n2, LANE), lambda bi, i: (i, 0, 0)),
                  pl.BlockSpec((1, 2, kb, n2, F_W), lambda bi, i: (bi, 0, i, 0, 0))],
        out_specs=pl.BlockSpec((1, n2, kb * 2 * F_W), lambda bi, i: (bi, 0, i)),
        out_shape=jax.ShapeDtypeStruct((b, n2, n1 * 2 * F_W), BF),
        compiler_params=_params(("parallel", "parallel")),
        name="fft_b",
    )(m2, ct, st, a.reshape(b, 2, n1, n2, F_W))
    return out.reshape(b, l, 2 * F_W)


def _factorizable(l):
    return l % (8 * FFT_N2) == 0


def _position_tables(l):
    return _fft_tables(l) if _factorizable(l) else _dft_tables(l)


def _position_dft(fa_x, tabs):
    return _fft(fa_x, *tabs) if fa_x.shape[2] != F_W else _dft(fa_x, *tabs)


def _gdn_prep_block(x_ref, prev_ref, next_ref, ab_ref, cw_ref, alog_ref, dtb_ref):
    i = pl.program_id(1)
    last = pl.num_programs(1) - 1
    x = x_ref[0].astype(F32)
    tm = x.shape[0]
    hr = prev_ref.shape[1]
    prev_row = prev_ref[0, hr - 1:hr, :].astype(F32) * (i > 0).astype(F32)
    next_row = next_ref[0, 0:1, :].astype(F32) * (i < last).astype(F32)
    rows = lax.broadcasted_iota(jnp.int32, x.shape, 0)
    x_dn = jnp.where(rows == 0, prev_row, pltpu.roll(x, 1, axis=0))
    x_up = jnp.where(rows == tm - 1, next_row, pltpu.roll(x, tm - 1, axis=0))
    cw = cw_ref[...]
    y = cw[0:1, :] * x_dn + cw[1:2, :] * x + cw[2:3, :] * x_up
    y = y * jax.nn.sigmoid(y)
    qn, kn = [], []
    for h in range(DN_H):
        qs = y[:, h * DN_DK:(h + 1) * DN_DK]
        ks = y[:, DN_W + h * DN_DK:DN_W + (h + 1) * DN_DK]
        qn.append((qs * lax.rsqrt(jnp.sum(qs * qs, axis=-1, keepdims=True) + EPS) * (DN_DK ** -0.5)).astype(BF))
        kn.append((ks * lax.rsqrt(jnp.sum(ks * ks, axis=-1, keepdims=True) + EPS)).astype(BF))
    a = ab_ref[0]
    cols = lax.broadcasted_iota(jnp.int32, a.shape, 1)
    z = a + dtb_ref[...]
    softplus = jnp.maximum(z, 0.0) + jnp.log1p(jnp.exp(-jnp.abs(z)))
    g = -jnp.exp(alog_ref[...]) * softplus
    gates = jnp.where(cols < 2 * DN_H, g, jnp.where(cols < 4 * DN_H, jax.nn.sigmoid(a), 0.0))
    return jnp.concatenate(qn, axis=1), jnp.concatenate(kn, axis=1), y[:, 2 * DN_W:].astype(BF), gates


def _nt_dot(a, b):
    return lax.dot_general(a, b, (((1,), (1,)), ((), ())), preferred_element_type=F32)


def _pair_rows(top, bot):
    z = jnp.zeros_like(top)
    return jnp.concatenate([jnp.concatenate([top, z], axis=1), jnp.concatenate([z, bot], axis=1)], axis=0)


def _gdn_local_kernel(x_ref, prev_ref, next_ref, ab_ref, cw_ref, alog_ref, dtb_ref,
                      a1_ref, a2_ref, u_ref, egl_ref, *, cg):
    c = DN_CHUNK
    sq_r = lax.broadcasted_iota(jnp.int32, (c, c), 0)
    sq_c = lax.broadcasted_iota(jnp.int32, (c, c), 1)
    tri = ((sq_r >= sq_c).astype(F32), (sq_r <= sq_c).astype(F32))
    ri = lax.broadcasted_iota(jnp.int32, (c, 2 * c), 0)
    lane = lax.broadcasted_iota(jnp.int32, (c, 2 * c), 1)
    cj = lane & (c - 1)
    first = lane < c
    wide_first = lax.broadcasted_iota(jnp.int32, (c, DN_PW), 1) < DN_DK
    eye = (ri == cj).astype(F32)
    lvl_masks = []
    bsz = 1
    while bsz < c:
        lvl_masks.append((ri // (2 * bsz) == cj // (2 * bsz)) & (ri // bsz != cj // bsz))
        bsz *= 2
    incl = (ri >= cj, ri <= cj)
    strict = (ri > cj, ri < cj)
    last = (c - 1, 0)
    bd_r = lax.broadcasted_iota(jnp.int32, (2 * c, 2 * c), 0) // c
    bd_c = lax.broadcasted_iota(jnp.int32, (2 * c, 2 * c), 1) // c
    bd = bd_r == bd_c

    def bdiag(x):
        return jnp.where(bd, jnp.concatenate([x, x], axis=0), 0.0)

    def spread(x, ca, cb, sel):
        return jnp.where(sel, x[:, ca:ca + 1], x[:, cb:cb + 1])

    qn, kn, vv, gates = _gdn_prep_block(x_ref, prev_ref, next_ref, ab_ref, cw_ref, alog_ref, dtb_ref)
    systems = []
    for cc in range(cg):
        rows = slice(cc * c, (cc + 1) * c)
        ga = gates[rows, :]
        pairs = []
        for p in range(DN_PAIRS):
            cols = slice(p * DN_PW, (p + 1) * DN_PW)
            q2, k2, v2 = qn[rows, cols], kn[rows, cols], vv[rows, cols]
            bdk = _pair_rows(k2[:, :DN_DK], k2[:, DN_DK:])
            pairs.append((q2, k2, v2, _nt_dot(k2, bdk), _nt_dot(q2, bdk)))
        for d in range(2):
            gcum = jnp.dot(tri[d], ga, precision=HIGHEST, preferred_element_type=F32)
            gcum_t = jnp.concatenate([gcum, gcum], axis=0).T
            glast = gcum[last[d]:last[d] + 1, :]
            for p in range(DN_PAIRS):
                q2, k2, v2, kk, qk = pairs[p]
                ca, cb = d * DN_H + 2 * p, d * DN_H + 2 * p + 1
                gc = spread(gcum, ca, cb, first)
                gr = jnp.where(first[:1], gcum_t[ca:ca + 1, :], gcum_t[cb:cb + 1, :])
                bc = spread(ga, 2 * DN_H + ca, 2 * DN_H + cb, first)
                decay = jnp.where(incl[d], jnp.exp(jnp.where(incl[d], gc - gr, 0.0)), 0.0)
                a_m = jnp.where(strict[d], bc * kk * decay, 0.0)
                gcw = spread(gcum, ca, cb, wide_first)
                bcw = spread(ga, 2 * DN_H + ca, 2 * DN_H + cb, wide_first)
                glw = spread(glast, ca, cb, wide_first[:1])
                systems.append(dict(idx=(d, cc, p), q=q2.astype(F32), k=k2.astype(F32), v=v2.astype(F32),
                                    bcw=bcw, egw=jnp.exp(gcw), kdw=jnp.exp(glw - gcw), eglw=jnp.exp(glw),
                                    qk=qk * decay, a_m=a_m, t=eye - jnp.where(lvl_masks[0], a_m, 0.0)))
    for lm in lvl_masks[1:]:
        inner = [_dot(jnp.where(lm, s["a_m"], 0.0), bdiag(s["t"])) for s in systems]
        for s, m in zip(systems, inner):
            s["t"] = s["t"] - _dot(s["t"], bdiag(m))
    sols = []
    for s in systems:
        wk = (s["bcw"] * s["egw"]) * s["k"]
        uv = s["bcw"] * s["v"]
        z = jnp.zeros_like(wk[:, :DN_DK])
        rhs = jnp.concatenate([jnp.concatenate([wk[:, :DN_DK], uv[:, :DN_DK], z, z], axis=1),
                               jnp.concatenate([z, z, wk[:, DN_DK:], uv[:, DN_DK:]], axis=1)], axis=0)
        sols.append(_dot(s["t"], rhs))
    for s, sol in zip(systems, sols):
        d, cc, p = s["idx"]
        kd = s["k"] * s["kdw"]
        a1_ref[0, d, cc, p, :c, :] = jnp.concatenate([sol[:, :DN_DK], sol[:, 2 * DN_DK:3 * DN_DK]], axis=1).astype(BF)
        a1_ref[0, d, cc, p, c:, :] = (s["q"] * s["egw"]).astype(BF)
        a2_ref[0, d, cc, p, :c, :] = s["qk"].astype(BF)
        a2_ref[0, d, cc, p, c:, :] = jnp.concatenate([kd[:, :DN_DK], kd[:, DN_DK:]], axis=0).T.astype(BF)
        u_ref[0, d, cc, p] = jnp.concatenate([sol[:, DN_DK:2 * DN_DK], sol[:, 3 * DN_DK:]], axis=1).astype(BF)
        egl_ref[0, d, cc, p:p + 1, :] = s["eglw"]


def _gdn_local(qkv, ab, conv_w, a_log, dt_bias):
    b, l, w = qkv.shape
    n = l // DN_CHUNK
    cg = max(g for g in (1, 2, 4, 8) if n % g == 0)
    c = DN_CHUNK
    tm = cg * c
    hr = 16
    nb = tm // hr
    last_blk = l // hr - 1
    pad = jnp.zeros((1, LANE - 2 * DN_H), F32)
    alog = jnp.concatenate([a_log.reshape(1, 2 * DN_H).astype(F32), pad], axis=1)
    dtb = jnp.concatenate([dt_bias.reshape(1, 2 * DN_H).astype(F32), pad], axis=1)
    out = lambda r, w_: pl.BlockSpec((1, 2, cg, DN_PAIRS, r, w_), lambda bi, i: (bi, 0, i, 0, 0, 0))
    return pl.pallas_call(
        functools.partial(_gdn_local_kernel, cg=cg),
        grid=(b, n // cg),
        in_specs=[pl.BlockSpec((1, tm, w), lambda bi, i: (bi, i, 0)),
                  pl.BlockSpec((1, hr, w), lambda bi, i: (bi, jnp.maximum(i * nb - 1, 0), 0)),
                  pl.BlockSpec((1, hr, w), lambda bi, i: (bi, jnp.minimum((i + 1) * nb, last_blk), 0)),
                  pl.BlockSpec((1, tm, LANE), lambda bi, i: (bi, i, 0)),
                  pl.BlockSpec((3, w), lambda bi, i: (0, 0)),
                  pl.BlockSpec((1, LANE), lambda bi, i: (0, 0)),
                  pl.BlockSpec((1, LANE), lambda bi, i: (0, 0))],
        out_specs=[out(2 * c, DN_PW), out(c + DN_DK, 2 * c), out(c, DN_PW),
                   pl.BlockSpec((1, 2, cg, DN_PAIRS, DN_PW), lambda bi, i: (bi, 0, i, 0, 0))],
        out_shape=[jax.ShapeDtypeStruct((b, 2, n, DN_PAIRS, 2 * c, DN_PW), BF),
                   jax.ShapeDtypeStruct((b, 2, n, DN_PAIRS, c + DN_DK, 2 * c), BF),
                   jax.ShapeDtypeStruct((b, 2, n, DN_PAIRS, c, DN_PW), BF),
                   jax.ShapeDtypeStruct((b, 2, n, DN_PAIRS, DN_PW), F32)],
        compiler_params=_params(("parallel", "parallel")),
        name="gdn_local",
    )(qkv, qkv, qkv, ab, conv_w.astype(F32), alog, dtb)


def _gdn_scan_kernel(a1f, a2f, uf, ef, a1b, a2b, ub, eb, s0_ref, of_ref, ob_ref, s_ref, *, bg):
    j = pl.program_id(1)

    @pl.when(j == 0)
    def _():
        s_ref[...] = s0_ref[...]

    c = DN_CHUNK
    dirs = ((a1f, a2f, uf, ef, of_ref), (a1b, a2b, ub, eb, ob_ref))
    chains = [(bb, d, p) for bb in range(bg) for d in range(2) for p in range(DN_PAIRS)]
    halves = lambda x: _pair_rows(x[:, :DN_DK], x[:, DN_DK:]).astype(BF)
    st = [s_ref[bb, d, p] for bb, d, p in chains]
    r1 = [jnp.dot(dirs[d][0][bb, 0, 0, p], halves(s), preferred_element_type=F32)
          for (bb, d, p), s in zip(chains, st)]
    un = [dirs[d][2][bb, 0, 0, p].astype(F32) - r[:c] for (bb, d, p), r in zip(chains, r1)]
    r2 = [jnp.dot(dirs[d][1][bb, 0, 0, p], halves(u), preferred_element_type=F32)
          for (bb, d, p), u in zip(chains, un)]
    for (bb, d, p), s, ra, rb in zip(chains, st, r1, r2):
        dirs[d][4][bb, :, p * DN_PW:(p + 1) * DN_PW] = (ra[c:] + rb[:c]).astype(of_ref.dtype)
        s_ref[bb, d, p] = s * dirs[d][3][bb, 0, 0, p:p + 1, :] + rb[c:]


def _gdn_scan(a1, a2, u, egl, s0):
    b, _, n, _, _, _ = a1.shape
    c = DN_CHUNK
    bg = max(g for g in (1, 2, 4, 8) if b % g == 0)
    fwd = lambda bi, j: (bi, 0, j, 0, 0, 0)
    bwd = lambda bi, j: (bi, 1, n - 1 - j, 0, 0, 0)
    blk = lambda a, m: pl.BlockSpec((bg, 1, 1) + a.shape[3:], m)
    eblk = lambda m: pl.BlockSpec((bg, 1, 1, DN_PAIRS, DN_PW), lambda bi, j: m(bi, j)[:5])
    st = pl.BlockSpec((bg, 2, DN_PAIRS, DN_DK, DN_PW), lambda bi, j: (bi, 0, 0, 0, 0))
    return pl.pallas_call(
        functools.partial(_gdn_scan_kernel, bg=bg),
        grid=(b // bg, n),
        in_specs=[blk(a1, fwd), blk(a2, fwd), blk(u, fwd), eblk(fwd),
                  blk(a1, bwd), blk(a2, bwd), blk(u, bwd), eblk(bwd), st],
        out_specs=[pl.BlockSpec((bg, c, DN_W), lambda bi, j: (bi, j, 0)),
                   pl.BlockSpec((bg, c, DN_W), lambda bi, j: (bi, n - 1 - j, 0)), st],
        out_shape=[jax.ShapeDtypeStruct((b, n * c, DN_W), BF), jax.ShapeDtypeStruct((b, n * c, DN_W), BF),
                   jax.ShapeDtypeStruct((b, 2, DN_PAIRS, DN_DK, DN_PW), F32)],
        compiler_params=_params(("parallel", "arbitrary")),
        name="gdn_scan",
    )(a1, a2, u, egl, a1, a2, u, egl, s0)


def _mla_prep_kernel(m_ref, qn_ref, kvn_ref, wq_ref, wqr_ref, wk_ref, wv_ref, cos_ref, sin_ref,
                     q_ref, k_ref, v_ref):
    m = m_ref[0].astype(F32)
    cq = m[:, :M_QL]
    ckv = m[:, M_QL:M_QL + M_KVL]
    kpa = m[:, M_QL + M_KVL:M_QL + M_KVL + LANE]
    kpb = m[:, M_QL + M_KVL + LANE:]
    hq = (cq * lax.rsqrt(jnp.mean(cq * cq, axis=-1, keepdims=True) + EPS) * qn_ref[...]).astype(BF)
    hkv = (ckv * lax.rsqrt(jnp.mean(ckv * ckv, axis=-1, keepdims=True) + EPS) * kvn_ref[...]).astype(BF)
    cos = cos_ref[...]
    sin = sin_ref[...]
    scale = (M_NOPE + M_ROPE) ** -0.5 * math.log2(math.e)
    q_all = jnp.dot(hq, wq_ref[...], preferred_element_type=F32)
    q_rot = jnp.dot(hq, wqr_ref[...], preferred_element_type=F32)
    k_all = jnp.dot(hkv, wk_ref[...], preferred_element_type=F32)
    v_all_t = _nt_dot(wv_ref[...], hkv)
    kpe = kpa * cos + kpb * sin
    rows = lax.broadcasted_iota(jnp.int32, (M_HP, hkv.shape[0]), 0)
    one_row = jnp.where(rows == M_V, 1.0, 0.0)
    for h in range(M_H):
        sl = slice(h * M_HP, (h + 1) * M_HP)
        q_ref[0, h] = ((q_all[:, sl] * cos + q_rot[:, sl] * sin) * scale).astype(q_ref.dtype)
        k_ref[0, h] = (k_all[:, sl] + kpe).astype(k_ref.dtype)
        v_ref[0, h] = (v_all_t[sl, :] + one_row).astype(v_ref.dtype)


def _mla_prep(mla_in, q_norm, kv_norm, wq, wqr, wk, wv, cos_t, sin_t):
    b, l, w = mla_in.shape
    tm = _tile(l, 512)
    full = lambda a: pl.BlockSpec(a.shape, lambda bi, i: (0,) * a.ndim)
    wvt = wv.T
    return pl.pallas_call(
        _mla_prep_kernel,
        grid=(b, l // tm),
        in_specs=[pl.BlockSpec((1, tm, w), lambda bi, i: (bi, i, 0)),
                  pl.BlockSpec((1, M_QL), lambda bi, i: (0, 0)),
                  pl.BlockSpec((1, M_KVL), lambda bi, i: (0, 0)),
                  full(wq), full(wqr), full(wk), full(wvt),
                  pl.BlockSpec((tm, LANE), lambda bi, i: (i, 0)),
                  pl.BlockSpec((tm, LANE), lambda bi, i: (i, 0))],
        out_specs=[pl.BlockSpec((1, M_H, tm, M_HP), lambda bi, i: (bi, 0, i, 0))] * 2
                  + [pl.BlockSpec((1, M_H, M_HP, tm), lambda bi, i: (bi, 0, 0, i))],
        out_shape=[jax.ShapeDtypeStruct((b, M_H, l, M_HP), BF)] * 2
                  + [jax.ShapeDtypeStruct((b, M_H, M_HP, l), BF)],
        compiler_params=_params(("parallel", "parallel")),
        name="mla_prep",
    )(mla_in, q_norm.reshape(1, M_QL).astype(F32), kv_norm.reshape(1, M_KVL).astype(F32),
      wq, wqr, wk, wvt, cos_t, sin_t)


def _attn_kernel(q_ref, *refs, n_src):
    k_refs, v_refs, o_ref = refs[:n_src], refs[n_src:2 * n_src], refs[2 * n_src]
    tq = q_ref.shape[2]
    sub = min(tq, ATT_SUB)
    units = [(hh, r0) for hh in range(2) for r0 in range(0, tq, sub)]

    tiles = [(src, k0, min(ATT_KT, k_r.shape[2] - k0))
             for src, k_r in enumerate(k_refs) for k0 in range(0, k_r.shape[2], ATT_KT)]

    def score_tile(u, tile):
        (hh, r0), (src, k0, ksz) = u, tile
        return _nt_dot(k_refs[src][0, hh, k0:k0 + ksz, :], q_ref[0, hh, r0:r0 + sub, :])

    def value_tile(u, tile, s_t, m):
        (hh, _), (src, k0, ksz) = u, tile
        return jnp.dot(v_refs[src][0, hh, :ATT_VROWS, k0:k0 + ksz], jnp.exp2(s_t - m).astype(BF),
                       preferred_element_type=F32)

    s_cur = [score_tile(units[0], t) for t in tiles]
    for i, u in enumerate(units):
        s_nxt = [score_tile(units[i + 1], t) for t in tiles] if i + 1 < len(units) else None
        m = s_cur[0].max(axis=0, keepdims=True)
        for s_t in s_cur[1:]:
            m = jnp.maximum(m, s_t.max(axis=0, keepdims=True))
        acc = None
        for t, s_t in zip(tiles, s_cur):
            part = value_tile(u, t, s_t, m)
            acc = part if acc is None else acc + part
        hh, r0 = u
        o = acc[:M_V] / acc[M_V:M_V + 1]
        o_ref[0, r0:r0 + sub, hh * M_V:(hh + 1) * M_V] = o.T.astype(o_ref.dtype)
        s_cur = s_nxt


def _attention(q, ks, vs):
    b, _, l, _ = q.shape
    tq = _tile(l, ATT_UNITS * ATT_SUB)
    n_src = len(ks)
    kv_spec = lambda a: pl.BlockSpec((1, 2) + a.shape[2:], lambda bi, hp, i: (bi, hp, 0, 0))
    return pl.pallas_call(
        functools.partial(_attn_kernel, n_src=n_src),
        grid=(b, M_H // 2, l // tq),
        in_specs=[pl.BlockSpec((1, 2, tq, M_HP), lambda bi, hp, i: (bi, hp, i, 0))]
                 + [kv_spec(a) for a in ks] + [kv_spec(a) for a in vs],
        out_specs=pl.BlockSpec((1, tq, 2 * M_V), lambda bi, hp, i: (bi, i, hp)),
        out_shape=jax.ShapeDtypeStruct((b, l, M_H * M_V), BF),
        compiler_params=_params(("parallel", "parallel", "arbitrary")),
        name="attention",
    )(q, *ks, *vs)


def _merge_kernel(pq_ref, faz_ref, of_ref, ob_ref, dnz_ref, om_ref, mz_ref, gl_ref, x_ref, gm_ref, gp_ref,
                  fw_ref, cc_ref, sc_ref, dnn_ref, wb_ref, wo_ref, o_ref):
    pq = pq_ref[0]
    faz = _silu(faz_ref[0]).astype(F32)
    ya = []
    for g in range(F_GROUPS):
        sl = slice(g * F_GD, (g + 1) * F_GD)
        spec = (jnp.dot(pq[:, sl], cc_ref[...], preferred_element_type=F32)
                - jnp.dot(pq[:, F_W + g * F_GD:F_W + (g + 1) * F_GD], sc_ref[...], preferred_element_type=F32))
        t = jnp.dot(spec.astype(BF), fw_ref[g], preferred_element_type=F32)
        ya.append((t * faz[:, sl]).astype(BF))
    ya = jnp.concatenate(ya, axis=1)
    osum = of_ref[0].astype(F32) + ob_ref[0].astype(F32)
    dnz = _silu(dnz_ref[0]).astype(F32)
    yb = []
    for h in range(DN_H):
        sl = slice(h * DN_DK, (h + 1) * DN_DK)
        seg = osum[:, sl]
        nrm = seg * lax.rsqrt(jnp.mean(seg * seg, axis=-1, keepdims=True) + EPS) * dnn_ref[...]
        yb.append((nrm * dnz[:, sl]).astype(BF))
    yb = jnp.concatenate(yb, axis=1)
    yc = om_ref[0] * _silu(mz_ref[0])
    merged = None
    for idx, yy in enumerate((ya, yb, yc)):
        gate = _sigmoid(gl_ref[0, :, idx * D:(idx + 1) * D]).astype(F32)
        term = gate * jnp.dot(yy, wb_ref[idx], preferred_element_type=F32)
        merged = term if merged is None else merged + term
    y = jnp.dot(merged.astype(BF), wo_ref[...], preferred_element_type=F32)
    yn = y * lax.rsqrt(jnp.mean(y * y, axis=-1, keepdims=True) + EPS) * gp_ref[...]
    o_ref[0] = x_ref[0] + gm_ref[0] * yn


def _merge(pq, fa_z, o_f, o_b, dn_z, o_m, mla_z, gate_logits, x, gate_mod, g_post, fw, cc, sc, dn_norm, wb, wo,
           per_batch_mod):
    b, l, _ = x.shape
    tm = _tile(l, 512)
    row = lambda w: pl.BlockSpec((1, tm, w), lambda bi, i: (bi, i, 0))
    full = lambda a: pl.BlockSpec(a.shape, lambda bi, i: (0,) * a.ndim)
    mod_map = (lambda bi, i: (bi, 0, 0)) if per_batch_mod else (lambda bi, i: (0, 0, 0))
    gp = g_post.reshape(1, D).astype(F32)
    dnn = dn_norm.reshape(1, DN_DK).astype(F32)
    return pl.pallas_call(
        _merge_kernel,
        grid=(b, l // tm),
        in_specs=[row(2 * F_W), row(F_W), row(DN_W), row(DN_W), row(DN_W), row(512), row(512), row(3 * D), row(D),
                  pl.BlockSpec((1, 1, D), mod_map), full(gp), full(fw), full(cc), full(sc), full(dnn),
                  full(wb), full(wo)],
        out_specs=row(D),
        out_shape=jax.ShapeDtypeStruct((b, l, D), F32),
        compiler_params=_params(("parallel", "parallel")),
        name="merge",
    )(pq, fa_z, o_f, o_b, dn_z, o_m, mla_z, gate_logits, x, gate_mod, gp, fw, cc, sc, dnn, wb, wo)


def _rot_cols(w):
    q = M_ROPE // 4
    return jnp.concatenate([-w[..., q:2 * q], w[..., :q], -w[..., 3 * q:], w[..., 2 * q:3 * q]], axis=-1)


def _layer_weights(w_in, mla_w_uq, mla_w_ukv):
    col = lambda n: w_in[:, _OFF[n][0]:_OFF[n][1]]
    zeros = lambda n: jnp.zeros((D, n), w_in.dtype)
    w_qkv = jnp.concatenate([col("dn_q"), col("dn_k"), col("dn_v")], axis=1)
    w_ab = jnp.concatenate([col("dn_ab"), zeros(LANE - 4 * DN_H)], axis=1)
    kpe = col("kpe")
    rest = LANE - M_NOPE - M_ROPE
    kpe_a = jnp.concatenate([zeros(M_NOPE), kpe, zeros(rest)], axis=1)
    kpe_b = jnp.concatenate([zeros(M_NOPE), _rot_cols(kpe), zeros(rest)], axis=1)
    w_mla = jnp.concatenate([col("cq"), col("ckv"), kpe_a, kpe_b], axis=1)
    grp1 = [col("fa_x"), col("fa_z"), w_qkv, col("dn_z"), w_ab]
    grp2 = [w_mla, col("mla_z"), col("gate")]
    uq = mla_w_uq.reshape(M_QL, M_H, M_NOPE + M_ROPE)
    zq = jnp.zeros((M_QL, M_H, rest), uq.dtype)
    wq = jnp.concatenate([uq, zq], axis=-1).reshape(M_QL, M_H * M_HP)
    wqr = jnp.concatenate([jnp.zeros((M_QL, M_H, M_NOPE), uq.dtype), _rot_cols(uq[..., M_NOPE:]), zq],
                          axis=-1).reshape(M_QL, M_H * M_HP)
    ukv = mla_w_ukv.reshape(M_KVL, M_H, M_NOPE + M_V)
    zk = jnp.zeros((M_KVL, M_H, M_HP - M_NOPE), ukv.dtype)
    wk = jnp.concatenate([ukv[..., :M_NOPE], zk], axis=-1).reshape(M_KVL, M_H * M_HP)
    wv = jnp.concatenate([ukv[..., M_NOPE:], jnp.zeros((M_KVL, M_H, M_HP - M_V), ukv.dtype)],
                         axis=-1).reshape(M_KVL, M_H * M_HP)
    cast = lambda ws: [w.astype(BF) for w in ws]
    return cast(grp1), cast(grp2), cast([wq, wqr, wk, wv])


def _rope_tables(l, rotary):
    ones = jnp.ones((l, M_NOPE), F32)
    pad = jnp.zeros((l, LANE - M_NOPE - M_ROPE), F32)
    if not rotary:
        return (jnp.concatenate([ones, jnp.ones((l, M_ROPE), F32), pad], axis=1),
                jnp.zeros((l, LANE), F32))
    pos = jnp.arange(l, dtype=jnp.int32)
    row = (pos // GRID_W).astype(F32)
    colp = (pos % GRID_W).astype(F32)
    n_freq = M_ROPE // 4
    inv = ROPE_BASE ** (-jnp.arange(n_freq, dtype=F32) / n_freq)
    ang_r = row[:, None] * inv
    ang_c = colp[:, None] * inv
    ang = jnp.concatenate([ang_r, ang_r, ang_c, ang_c], axis=-1)
    return (jnp.concatenate([ones, jnp.cos(ang), pad], axis=1),
            jnp.concatenate([jnp.zeros((l, M_NOPE), F32), jnp.sin(ang), pad], axis=1))


def _chan_tables():
    j = jnp.arange(F_GD, dtype=jnp.int32)
    a = ((j[:, None] * j[None, :]) % F_GD).astype(F32) * (2.0 * math.pi / F_GD)
    nrm = F_GD ** -0.5
    return (jnp.cos(a) * nrm).astype(BF), (jnp.sin(a) * nrm).astype(BF)


def _layer(x, ctx, mod, p, tables, need_ctx_out):
    b = x.shape[0]
    shift, scale, gate = (mod[:b, None, i * D:(i + 1) * D] for i in range(3))
    shift_c, scale_c, gate_c = (mod[b:b + 1, None, i * D:(i + 1) * D] for i in range(3))
    grp1, grp2, (wq, wqr, wk, wv) = _layer_weights(p["w_in"], p["mla_w_uq"], p["mla_w_ukv"])
    dts1 = [BF, BF, BF, BF, F32]
    dts2 = [BF, BF, BF]

    def project(t, sc, sh, per_batch):
        a = _inproj(t, sc, sh, p["g_pre"], grp1, dts1, per_batch, fold_first=_factorizable(t.shape[1]))
        bb = _inproj(t, sc, sh, p["g_pre"], grp2, dts2, per_batch)
        return a, bb

    (fa_x, fa_z, qkv, dn_z, ab), (mla_in, mla_z, gl) = project(x, scale, shift, True)
    (fa_xc, fa_zc, qkvc, dn_zc, abc), (mla_inc, mla_zc, glc) = project(ctx, scale_c, shift_c, False)

    gdn_args = (p["dn_conv"], p["dn_a_log"], p["dn_dt_bias"])
    s0 = jnp.zeros((b, 2, DN_PAIRS, DN_DK, DN_PW), F32)
    ofc, obc, s_ctx = _gdn_scan(*_gdn_local(qkvc, abc, *gdn_args), s0)
    of, ob, _ = _gdn_scan(*_gdn_local(qkv, ab, *gdn_args), s_ctx)

    q_m, k_m, v_m = _mla_prep(mla_in, p["mla_q_norm"], p["mla_kv_norm"], wq, wqr, wk, wv, *tables["rope"])
    qc_m, kc_m, vc_m = _mla_prep(mla_inc, p["mla_q_norm"], p["mla_kv_norm"], wq, wqr, wk, wv, *tables["rope_c"])
    o_m = _attention(q_m, [k_m, kc_m], [v_m, vc_m])

    pq = _position_dft(fa_x, tables["dft"])
    fw = p["f_w"].astype(BF)
    wb = p["w_branch"].astype(BF)
    wo = p["w_out"].astype(BF)
    cc, sc = tables["chan"]
    x_new = _merge(pq, fa_z, of, ob, dn_z, o_m, mla_z, gl, x, gate, p["g_post"], fw, cc, sc, p["dn_norm"],
                   wb, wo, True)
    ctx_new = ctx
    if need_ctx_out:
        pqc = _position_dft(fa_xc, tables["dft_c"])
        oc_m = _attention(qc_m, [kc_m], [vc_m])
        ctx_new = _merge(pqc, fa_zc, ofc, obc, dn_zc, oc_m, mla_zc, glc, ctx, gate_c, p["g_post"], fw, cc, sc,
                         p["dn_norm"], wb, wo, False)
    return x_new, ctx_new


def kernel(x, c, ctx, c_ctx, w_mod, b_mod, g_pre, g_post, w_in, f_w, dn_conv, dn_a_log, dn_dt_bias, dn_norm,
           mla_q_norm, mla_w_uq, mla_kv_norm, mla_w_ukv, w_branch, w_out):
    b, l, _ = x.shape
    lc = ctx.shape[1]
    depth = w_mod.shape[0]
    rows = -(-(b + 1) // 8) * 8
    c_all = jnp.concatenate([c, c_ctx[None, :], jnp.zeros((rows - b - 1, D), F32)], axis=0)
    tables = {"dft": _position_tables(l), "dft_c": _position_tables(lc), "chan": _chan_tables(),
              "rope": _rope_tables(l, True), "rope_c": _rope_tables(lc, False)}
    bf = lambda w: w.astype(BF)
    per_layer = dict(g_pre=g_pre, g_post=g_post, w_in=bf(w_in), f_w=bf(f_w), dn_conv=dn_conv, dn_a_log=dn_a_log,
                     dn_dt_bias=dn_dt_bias, dn_norm=dn_norm, mla_q_norm=mla_q_norm, mla_w_uq=bf(mla_w_uq),
                     mla_kv_norm=mla_kv_norm, mla_w_ukv=bf(mla_w_ukv), w_branch=bf(w_branch), w_out=bf(w_out))
    for li in range(depth):
        p = {k: v[li] for k, v in per_layer.items()}
        mod = _modulation(c_all, w_mod[li], b_mod[li])
        x, ctx = _layer(x, ctx, mod, p, tables, need_ctx_out=(li < depth - 1))
    return x
```

```python
import functools
import math

import jax
import jax.numpy as jnp
from jax import lax
from jax.experimental import pallas as pl
from jax.experimental.pallas import tpu as pltpu

F32 = jnp.float32
BF = jnp.bfloat16
HIGHEST = lax.Precision.HIGHEST

D = 1024
EPS = 1e-6
GRID_W = 64
F_GROUPS = 4
F_GD = 128
F_W = 512
DN_H = 4
DN_DK = 128
DN_W = 512
DN_CHUNK = 64
DN_PAIRS = DN_H // 2
DN_PW = 2 * DN_DK
M_H = 8
M_QL = 384
M_KVL = 256
M_NOPE = 64
M_ROPE = 32
M_V = 64
M_HP = 128
ROPE_BASE = 10000.0
LANE = 128
VMEM_LIMIT = 56 * 1024 * 1024
ATT_SUB = 512
ATT_UNITS = 8
ATT_KT = 4096
ATT_VROWS = 80

_OFF = {}
_o = 0
for _n, _w in (("fa_x", 512), ("fa_z", 512), ("dn_q", 512), ("dn_k", 512), ("dn_v", 512), ("dn_z", 512),
               ("dn_ab", 16), ("cq", M_QL), ("ckv", M_KVL), ("kpe", M_ROPE), ("mla_z", 512), ("gate", 3 * D)):
    _OFF[_n] = (_o, _o + _w)
    _o += _w


def _tile(n, pref, mult=16):
    t = min(n, pref)
    while t > mult and (n % t or t % mult):
        t -= mult
    assert n % t == 0, (n, pref)
    return t


def _params(sem):
    return pltpu.CompilerParams(dimension_semantics=sem, vmem_limit_bytes=VMEM_LIMIT)


def _sigmoid(v):
    return 0.5 * jnp.tanh(0.5 * v) + 0.5


def _silu(v):
    return v * _sigmoid(v)


def _dot(a, b):
    return jnp.dot(a.astype(BF), b.astype(BF), preferred_element_type=F32)


def _mod_kernel(c_ref, w_ref, b_ref, o_ref):
    c = c_ref[...]
    o_ref[...] = jnp.dot(_silu(c), w_ref[...], precision=HIGHEST, preferred_element_type=F32) + b_ref[...]


def _modulation(c_all, w_mod, b_mod):
    r = c_all.shape[0]
    tn = 512
    return pl.pallas_call(
        _mod_kernel,
        grid=(3 * D // tn,),
        in_specs=[pl.BlockSpec((r, D), lambda n: (0, 0)),
                  pl.BlockSpec((D, tn), lambda n: (0, n)),
                  pl.BlockSpec((1, tn), lambda n: (0, n))],
        out_specs=pl.BlockSpec((r, tn), lambda n: (0, n)),
        out_shape=jax.ShapeDtypeStruct((r, 3 * D), F32),
        compiler_params=_params(("arbitrary",)),
        name="modulation",
    )(c_all, w_mod, b_mod.reshape(1, 3 * D))


def _inproj_kernel(x_ref, sc_ref, sh_ref, g_ref, *refs, n_w, fold_first):
    w_refs, o_refs = refs[:n_w], refs[n_w:2 * n_w]
    x = x_ref[0]
    y = x * lax.rsqrt(jnp.mean(x * x, axis=-1, keepdims=True) + EPS) * g_ref[...]
    hb = (y * (1.0 + sc_ref[0]) + sh_ref[0]).astype(BF)
    for idx, (w_ref, o_ref) in enumerate(zip(w_refs, o_refs)):
        n = w_ref.shape[1]
        if fold_first and idx == 0:
            scr = refs[2 * n_w]
            res = jnp.dot(hb, w_ref[...], preferred_element_type=F32)
            groups = res.shape[0] // FFT_N2
            for j in range(n // LANE):
                for g in range(groups):
                    scr[j, g * FOLD_PITCH:g * FOLD_PITCH + FFT_N2, :] = (
                        res[g * FFT_N2:(g + 1) * FFT_N2, j * LANE:(j + 1) * LANE])
            for l2 in range(FFT_N2):
                for j in range(n // LANE):
                    o_ref[0, :, l2 * n + j * LANE:l2 * n + (j + 1) * LANE] = (
                        scr[j, pl.ds(l2, groups, stride=FOLD_PITCH), :].astype(o_ref.dtype))
            continue
        step = 512 if n % 512 == 0 else n
        for c0 in range(0, n, step):
            o_ref[0, :, c0:c0 + step] = jnp.dot(
                hb, w_ref[:, c0:c0 + step], preferred_element_type=F32).astype(o_ref.dtype)


def _inproj(x, scale, shift, g_pre, weights, out_dtypes, per_batch_mod, fold_first=False):
    b, l, _ = x.shape
    tm = _tile(l, 1024)
    mod_map = (lambda bi, i: (bi, 0, 0)) if per_batch_mod else (lambda bi, i: (0, 0, 0))
    in_specs = [pl.BlockSpec((1, tm, D), lambda bi, i: (bi, i, 0)),
                pl.BlockSpec((1, 1, D), mod_map),
                pl.BlockSpec((1, 1, D), mod_map),
                pl.BlockSpec((1, D), lambda bi, i: (0, 0))]
    in_specs += [pl.BlockSpec(w.shape, lambda bi, i: (0, 0)) for w in weights]
    out_specs = [pl.BlockSpec((1, tm, w.shape[1]), lambda bi, i: (bi, i, 0)) for w in weights]
    out_shape = [jax.ShapeDtypeStruct((b, l, w.shape[1]), dt) for w, dt in zip(weights, out_dtypes)]
    scratch = []
    if fold_first:
        n0 = weights[0].shape[1]
        out_specs[0] = pl.BlockSpec((1, tm // FFT_N2, FFT_N2 * n0), lambda bi, i: (bi, i, 0))
        out_shape[0] = jax.ShapeDtypeStruct((b, l // FFT_N2, FFT_N2 * n0), out_dtypes[0])
        scratch = [pltpu.VMEM((n0 // LANE, (tm // FFT_N2) * FOLD_PITCH, LANE), F32)]
    return pl.pallas_call(
        functools.partial(_inproj_kernel, n_w=len(weights), fold_first=fold_first),
        grid=(b, l // tm),
        in_specs=in_specs, out_specs=out_specs, out_shape=out_shape, scratch_shapes=scratch,
        compiler_params=_params(("parallel", "parallel")),
        name="inproj",
    )(x, scale, shift, g_pre.reshape(1, D), *weights)


def _dft_kernel(c_ref, s_ref, x_ref, o_ref, accp, accq):
    k = pl.program_id(2)

    @pl.when(k == 0)
    def _():
        accp[...] = jnp.zeros_like(accp)
        accq[...] = jnp.zeros_like(accq)

    xk = x_ref[0]
    accp[...] += jnp.dot(c_ref[...], xk, preferred_element_type=F32)
    accq[...] += jnp.dot(s_ref[...], xk, preferred_element_type=F32)

    @pl.when(k == pl.num_programs(2) - 1)
    def _():
        o_ref[0, :, :F_W] = accp[...].astype(o_ref.dtype)
        o_ref[0, :, F_W:] = accq[...].astype(o_ref.dtype)


def _dft_tables(l):
    blk = min(l, 64)
    j1 = jnp.arange(l // blk, dtype=jnp.int32)[:, None] * blk
    j2 = jnp.arange(blk, dtype=jnp.int32)[:, None]
    k = jnp.arange(l, dtype=jnp.int32)[None, :]
    w = 2.0 * math.pi / l
    a = ((j1 * k) % l).astype(F32) * w
    bb = ((j2 * k) % l).astype(F32) * w
    ca, sa, cb, sb = jnp.cos(a), jnp.sin(a), jnp.cos(bb), jnp.sin(bb)
    nrm = l ** -0.5
    cos = (ca[:, None, :] * cb[None] - sa[:, None, :] * sb[None]).reshape(l, l) * nrm
    sin = (sa[:, None, :] * cb[None] + ca[:, None, :] * sb[None]).reshape(l, l) * nrm
    return cos.astype(BF), sin.astype(BF)


def _dft(fa_x, cos_t, sin_t):
    b, l, _ = fa_x.shape
    tm = _tile(l, 1024)
    tk = _tile(l, 1024)
    return pl.pallas_call(
        _dft_kernel,
        grid=(b, l // tm, l // tk),
        in_specs=[pl.BlockSpec((tm, tk), lambda bi, i, k: (i, k)),
                  pl.BlockSpec((tm, tk), lambda bi, i, k: (i, k)),
                  pl.BlockSpec((1, tk, F_W), lambda bi, i, k: (bi, k, 0))],
        out_specs=pl.BlockSpec((1, tm, 2 * F_W), lambda bi, i, k: (bi, i, 0)),
        out_shape=jax.ShapeDtypeStruct((b, l, 2 * F_W), BF),
        scratch_shapes=[pltpu.VMEM((tm, F_W), F32), pltpu.VMEM((tm, F_W), F32)],
        compiler_params=_params(("parallel", "parallel", "arbitrary")),
        name="dft",
    )(cos_t, sin_t, fa_x)


FFT_N2 = 64
FOLD_PITCH = 72


def _fft_a_kernel(m_ref, x_ref, o_ref):
    n1 = x_ref.shape[1]
    r = jnp.dot(m_ref[...], x_ref[0], preferred_element_type=F32)
    o_ref[0, 0] = r[:n1].astype(o_ref.dtype)
    o_ref[0, 1] = r[n1:].astype(o_ref.dtype)


def _fft_b_kernel(m_ref, ct_ref, st_ref, a_ref, o_ref):
    kb = a_ref.shape[2]
    reps = F_W // LANE
    for j in range(kb):
        ar = a_ref[0, 0, j].astype(F32)
        ai = a_ref[0, 1, j].astype(F32)
        ct = jnp.concatenate([ct_ref[j]] * reps, axis=1)
        st = jnp.concatenate([st_ref[j]] * reps, axis=1)
        bri = jnp.concatenate([ar * ct + ai * st, ai * ct - ar * st], axis=0).astype(BF)
        r = jnp.dot(m_ref[...], bri, preferred_element_type=F32)
        o_ref[0, :, j * 2 * F_W:j * 2 * F_W + F_W] = r[:FFT_N2].astype(o_ref.dtype)
        o_ref[0, :, j * 2 * F_W + F_W:(j + 1) * 2 * F_W] = r[FFT_N2:].astype(o_ref.dtype)


def _fft_tables(l):
    n1, n2 = l // FFT_N2, FFT_N2
    ang = lambda a, bb, n: ((a[:, None] * bb[None, :]) % n).astype(F32) * (2.0 * math.pi / n)
    i1 = jnp.arange(n1, dtype=jnp.int32)
    i2 = jnp.arange(n2, dtype=jnp.int32)
    a1 = ang(i1, i1, n1)
    m1 = jnp.concatenate([jnp.cos(a1), -jnp.sin(a1)], axis=0).astype(BF)
    a2 = ang(i2, i2, n2)
    c2, s2 = jnp.cos(a2) * l ** -0.5, jnp.sin(a2) * l ** -0.5
    m2 = jnp.concatenate([jnp.concatenate([c2, s2], axis=1),
                          jnp.concatenate([s2, -c2], axis=1)], axis=0).astype(BF)
    at = ang(i1, i2, l)
    bc = lambda t: jnp.broadcast_to(t[:, :, None], (n1, n2, LANE))
    return m1, m2, bc(jnp.cos(at)), bc(jnp.sin(at))


def _fft(fa_x, m1, m2, ct, st):
    b, n1, wide = fa_x.shape
    n2 = FFT_N2
    l = n1 * n2
    tn = _tile(wide, 8192, LANE)
    a = pl.pallas_call(
        _fft_a_kernel,
        grid=(b, wide // tn),
        in_specs=[pl.BlockSpec((2 * n1, n1), lambda bi, i: (0, 0)),
                  pl.BlockSpec((1, n1, tn), lambda bi, i: (bi, 0, i))],
        out_specs=pl.BlockSpec((1, 2, n1, tn), lambda bi, i: (bi, 0, 0, i)),
        out_shape=jax.ShapeDtypeStruct((b, 2, n1, wide), BF),
        compiler_params=_params(("parallel", "parallel")),
        name="fft_a",
    )(m1, fa_x)
    kb = 8
    out = pl.pallas_call(
        _fft_b_kernel,
        grid=(b, n1 // kb),
        in_specs=[pl.BlockSpec((2 * n2, 2 * n2), lambda bi, i: (0, 0)),
                  pl.BlockSpec((kb, n2, LANE), lambda bi, i: (i, 0, 0)),
                  pl.BlockSpec((kb, n2, LANE), lambda bi, i: (i, 0, 0)),
                  pl.BlockSpec((1, 2, kb, n2, F_W), lambda bi, i: (bi, 0, i, 0, 0))],
        out_specs=pl.BlockSpec((1, n2, kb * 2 * F_W), lambda bi, i: (bi, 0, i)),
        out_shape=jax.ShapeDtypeStruct((b, n2, n1 * 2 * F_W), BF),
        compiler_params=_params(("parallel", "parallel")),
        name="fft_b",
    )(m2, ct, st, a.reshape(b, 2, n1, n2, F_W))
    return out.reshape(b, l, 2 * F_W)


def _factorizable(l):
    return l % (8 * FFT_N2) == 0


def _position_tables(l):
    return _fft_tables(l) if _factorizable(l) else _dft_tables(l)


def _position_dft(fa_x, tabs):
    return _fft(fa_x, *tabs) if fa_x.shape[2] != F_W else _dft(fa_x, *tabs)


def _gdn_prep_block(x_ref, prev_ref, next_ref, ab_ref, cw_ref, alog_ref, dtb_ref):
    i = pl.program_id(1)
    last = pl.num_programs(1) - 1
    x = x_ref[0].astype(F32)
    tm = x.shape[0]
    hr = prev_ref.shape[1]
    prev_row = prev_ref[0, hr - 1:hr, :].astype(F32) * (i > 0).astype(F32)
    next_row = next_ref[0, 0:1, :].astype(F32) * (i < last).astype(F32)
    rows = lax.broadcasted_iota(jnp.int32, x.shape, 0)
    x_dn = jnp.where(rows == 0, prev_row, pltpu.roll(x, 1, axis=0))
    x_up = jnp.where(rows == tm - 1, next_row, pltpu.roll(x, tm - 1, axis=0))
    cw = cw_ref[...]
    y = cw[0:1, :] * x_dn + cw[1:2, :] * x + cw[2:3, :] * x_up
    y = y * jax.nn.sigmoid(y)
    qn, kn = [], []
    for h in range(DN_H):
        qs = y[:, h * DN_DK:(h + 1) * DN_DK]
        ks = y[:, DN_W + h * DN_DK:DN_W + (h + 1) * DN_DK]
        qn.append((qs * lax.rsqrt(jnp.sum(qs * qs, axis=-1, keepdims=True) + EPS) * (DN_DK ** -0.5)).astype(BF))
        kn.append((ks * lax.rsqrt(jnp.sum(ks * ks, axis=-1, keepdims=True) + EPS)).astype(BF))
    a = ab_ref[0]
    cols = lax.broadcasted_iota(jnp.int32, a.shape, 1)
    z = a + dtb_ref[...]
    softplus = jnp.maximum(z, 0.0) + jnp.log1p(jnp.exp(-jnp.abs(z)))
    g = -jnp.exp(alog_ref[...]) * softplus
    gates = jnp.where(cols < 2 * DN_H, g, jnp.where(cols < 4 * DN_H, jax.nn.sigmoid(a), 0.0))
    return jnp.concatenate(qn, axis=1), jnp.concatenate(kn, axis=1), y[:, 2 * DN_W:].astype(BF), gates


def _nt_dot(a, b):
    return lax.dot_general(a, b, (((1,), (1,)), ((), ())), preferred_element_type=F32)


def _pair_rows(top, bot):
    z = jnp.zeros_like(top)
    return jnp.concatenate([jnp.concatenate([top, z], axis=1), jnp.concatenate([z, bot], axis=1)], axis=0)


def _gdn_local_kernel(x_ref, prev_ref, next_ref, ab_ref, cw_ref, alog_ref, dtb_ref,
                      a1_ref, a2_ref, u_ref, egl_ref, *, cg):
    c = DN_CHUNK
    sq_r = lax.broadcasted_iota(jnp.int32, (c, c), 0)
    sq_c = lax.broadcasted_iota(jnp.int32, (c, c), 1)
    tri = ((sq_r >= sq_c).astype(F32), (sq_r <= sq_c).astype(F32))
    ri = lax.broadcasted_iota(jnp.int32, (c, 2 * c), 0)
    lane = lax.broadcasted_iota(jnp.int32, (c, 2 * c), 1)
    cj = lane & (c - 1)
    first = lane < c
    wide_first = lax.broadcasted_iota(jnp.int32, (c, DN_PW), 1) < DN_DK
    eye = (ri == cj).astype(F32)
    lvl_masks = []
    bsz = 1
    while bsz < c:
        lvl_masks.append((ri // (2 * bsz) == cj // (2 * bsz)) & (ri // bsz != cj // bsz))
        bsz *= 2
    incl = (ri >= cj, ri <= cj)
    strict = (ri > cj, ri < cj)
    last = (c - 1, 0)
    bd_r = lax.broadcasted_iota(jnp.int32, (2 * c, 2 * c), 0) // c
    bd_c = lax.broadcasted_iota(jnp.int32, (2 * c, 2 * c), 1) // c
    bd = bd_r == bd_c

    def bdiag(x):
        return jnp.where(bd, jnp.concatenate([x, x], axis=0), 0.0)

    def spread(x, ca, cb, sel):
        return jnp.where(sel, x[:, ca:ca + 1], x[:, cb:cb + 1])

    qn, kn, vv, gates = _gdn_prep_block(x_ref, prev_ref, next_ref, ab_ref, cw_ref, alog_ref, dtb_ref)
    systems = []
    for cc in range(cg):
        rows = slice(cc * c, (cc + 1) * c)
        ga = gates[rows, :]
        pairs = []
        for p in range(DN_PAIRS):
            cols = slice(p * DN_PW, (p + 1) * DN_PW)
            q2, k2, v2 = qn[rows, cols], kn[rows, cols], vv[rows, cols]
            bdk = _pair_rows(k2[:, :DN_DK], k2[:, DN_DK:])
            pairs.append((q2, k2, v2, _nt_dot(k2, bdk), _nt_dot(q2, bdk)))
        for d in range(2):
            gcum = jnp.dot(tri[d], ga, precision=HIGHEST, preferred_element_type=F32)
            gcum_t = jnp.concatenate([gcum, gcum], axis=0).T
            glast = gcum[last[d]:last[d] + 1, :]
            for p in range(DN_PAIRS):
                q2, k2, v2, kk, qk = pairs[p]
                ca, cb = d * DN_H + 2 * p, d * DN_H + 2 * p + 1
                gc = spread(gcum, ca, cb, first)
                gr = jnp.where(first[:1], gcum_t[ca:ca + 1, :], gcum_t[cb:cb + 1, :])
                bc = spread(ga, 2 * DN_H + ca, 2 * DN_H + cb, first)
                decay = jnp.where(incl[d], jnp.exp(jnp.where(incl[d], gc - gr, 0.0)), 0.0)
                a_m = jnp.where(strict[d], bc * kk * decay, 0.0)
                gcw = spread(gcum, ca, cb, wide_first)
                bcw = spread(ga, 2 * DN_H + ca, 2 * DN_H + cb, wide_first)
                glw = spread(glast, ca, cb, wide_first[:1])
                systems.append(dict(idx=(d, cc, p), q=q2.astype(F32), k=k2.astype(F32), v=v2.astype(F32),
                                    bcw=bcw, egw=jnp.exp(gcw), kdw=jnp.exp(glw - gcw), eglw=jnp.exp(glw),
                                    qk=qk * decay, a_m=a_m, t=eye - jnp.where(lvl_masks[0], a_m, 0.0)))
    for lm in lvl_masks[1:]:
        inner = [_dot(jnp.where(lm, s["a_m"], 0.0), bdiag(s["t"])) for s in systems]
        for s, m in zip(systems, inner):
            s["t"] = s["t"] - _dot(s["t"], bdiag(m))
    sols = []
    for s in systems:
        wk = (s["bcw"] * s["egw"]) * s["k"]
        uv = s["bcw"] * s["v"]
        z = jnp.zeros_like(wk[:, :DN_DK])
        rhs = jnp.concatenate([jnp.concatenate([wk[:, :DN_DK], uv[:, :DN_DK], z, z], axis=1),
                               jnp.concatenate([z, z, wk[:, DN_DK:], uv[:, DN_DK:]], axis=1)], axis=0)
        sols.append(_dot(s["t"], rhs))
    for s, sol in zip(systems, sols):
        d, cc, p = s["idx"]
        kd = s["k"] * s["kdw"]
        a1_ref[0, d, cc, p, :c, :] = jnp.concatenate([sol[:, :DN_DK], sol[:, 2 * DN_DK:3 * DN_DK]], axis=1).astype(BF)
        a1_ref[0, d, cc, p, c:, :] = (s["q"] * s["egw"]).astype(BF)
        a2_ref[0, d, cc, p, :c, :] = s["qk"].astype(BF)
        a2_ref[0, d, cc, p, c:, :] = jnp.concatenate([kd[:, :DN_DK], kd[:, DN_DK:]], axis=0).T.astype(BF)
        u_ref[0, d, cc, p] = jnp.concatenate([sol[:, DN_DK:2 * DN_DK], sol[:, 3 * DN_DK:]], axis=1).astype(BF)
        egl_ref[0, d, cc, p:p + 1, :] = s["eglw"]


def _gdn_local(qkv, ab, conv_w, a_log, dt_bias):
    b, l, w = qkv.shape
    n = l // DN_CHUNK
    cg = max(g for g in (1, 2, 4, 8) if n % g == 0)
    c = DN_CHUNK
    tm = cg * c
    hr = 16
    nb = tm // hr
    last_blk = l // hr - 1
    pad = jnp.zeros((1, LANE - 2 * DN_H), F32)
    alog = jnp.concatenate([a_log.reshape(1, 2 * DN_H).astype(F32), pad], axis=1)
    dtb = jnp.concatenate([dt_bias.reshape(1, 2 * DN_H).astype(F32), pad], axis=1)
    out = lambda r, w_: pl.BlockSpec((1, 2, cg, DN_PAIRS, r, w_), lambda bi, i: (bi, 0, i, 0, 0, 0))
    return pl.pallas_call(
        functools.partial(_gdn_local_kernel, cg=cg),
        grid=(b, n // cg),
        in_specs=[pl.BlockSpec((1, tm, w), lambda bi, i: (bi, i, 0)),
                  pl.BlockSpec((1, hr, w), lambda bi, i: (bi, jnp.maximum(i * nb - 1, 0), 0)),
                  pl.BlockSpec((1, hr, w), lambda bi, i: (bi, jnp.minimum((i + 1) * nb, last_blk), 0)),
                  pl.BlockSpec((1, tm, LANE), lambda bi, i: (bi, i, 0)),
                  pl.BlockSpec((3, w), lambda bi, i: (0, 0)),
                  pl.BlockSpec((1, LANE), lambda bi, i: (0, 0)),
                  pl.BlockSpec((1, LANE), lambda bi, i: (0, 0))],
        out_specs=[out(2 * c, DN_PW), out(c + DN_DK, 2 * c), out(c, DN_PW),
                   pl.BlockSpec((1, 2, cg, DN_PAIRS, DN_PW), lambda bi, i: (bi, 0, i, 0, 0))],
        out_shape=[jax.ShapeDtypeStruct((b, 2, n, DN_PAIRS, 2 * c, DN_PW), BF),
                   jax.ShapeDtypeStruct((b, 2, n, DN_PAIRS, c + DN_DK, 2 * c), BF),
                   jax.ShapeDtypeStruct((b, 2, n, DN_PAIRS, c, DN_PW), BF),
                   jax.ShapeDtypeStruct((b, 2, n, DN_PAIRS, DN_PW), F32)],
        compiler_params=_params(("parallel", "parallel")),
        name="gdn_local",
    )(qkv, qkv, qkv, ab, conv_w.astype(F32), alog, dtb)


def _gdn_scan_kernel(a1f, a2f, uf, ef, a1b, a2b, ub, eb, s0_ref, of_ref, ob_ref, s_ref, *, bg):
    j = pl.program_id(1)

    @pl.when(j == 0)
    def _():
        s_ref[...] = s0_ref[...]

    c = DN_CHUNK
    dirs = ((a1f, a2f, uf, ef, of_ref), (a1b, a2b, ub, eb, ob_ref))
    chains = [(bb, d, p) for bb in range(bg) for d in range(2) for p in range(DN_PAIRS)]
    halves = lambda x: _pair_rows(x[:, :DN_DK], x[:, DN_DK:]).astype(BF)
    st = [s_ref[bb, d, p] for bb, d, p in chains]
    r1 = [jnp.dot(dirs[d][0][bb, 0, 0, p], halves(s), preferred_element_type=F32)
          for (bb, d, p), s in zip(chains, st)]
    un = [dirs[d][2][bb, 0, 0, p].astype(F32) - r[:c] for (bb, d, p), r in zip(chains, r1)]
    r2 = [jnp.dot(dirs[d][1][bb, 0, 0, p], halves(u), preferred_element_type=F32)
          for (bb, d, p), u in zip(chains, un)]
    for (bb, d, p), s, ra, rb in zip(chains, st, r1, r2):
        dirs[d][4][bb, :, p * DN_PW:(p + 1) * DN_PW] = (ra[c:] + rb[:c]).astype(of_ref.dtype)
        s_ref[bb, d, p] = s * dirs[d][3][bb, 0, 0, p:p + 1, :] + rb[c:]


def _gdn_scan(a1, a2, u, egl, s0):
    b, _, n, _, _, _ = a1.shape
    c = DN_CHUNK
    bg = max(g for g in (1, 2, 4, 8) if b % g == 0)
    fwd = lambda bi, j: (bi, 0, j, 0, 0, 0)
    bwd = lambda bi, j: (bi, 1, n - 1 - j, 0, 0, 0)
    blk = lambda a, m: pl.BlockSpec((bg, 1, 1) + a.shape[3:], m)
    eblk = lambda m: pl.BlockSpec((bg, 1, 1, DN_PAIRS, DN_PW), lambda bi, j: m(bi, j)[:5])
    st = pl.BlockSpec((bg, 2, DN_PAIRS, DN_DK, DN_PW), lambda bi, j: (bi, 0, 0, 0, 0))
    return pl.pallas_call(
        functools.partial(_gdn_scan_kernel, bg=bg),
        grid=(b // bg, n),
        in_specs=[blk(a1, fwd), blk(a2, fwd), blk(u, fwd), eblk(fwd),
                  blk(a1, bwd), blk(a2, bwd), blk(u, bwd), eblk(bwd), st],
        out_specs=[pl.BlockSpec((bg, c, DN_W), lambda bi, j: (bi, j, 0)),
                   pl.BlockSpec((bg, c, DN_W), lambda bi, j: (bi, n - 1 - j, 0)), st],
        out_shape=[jax.ShapeDtypeStruct((b, n * c, DN_W), BF), jax.ShapeDtypeStruct((b, n * c, DN_W), BF),
                   jax.ShapeDtypeStruct((b, 2, DN_PAIRS, DN_DK, DN_PW), F32)],
        compiler_params=_params(("parallel", "arbitrary")),
        name="gdn_scan",
    )(a1, a2, u, egl, a1, a2, u, egl, s0)


def _mla_prep_kernel(m_ref, qn_ref, kvn_ref, wq_ref, wqr_ref, wk_ref, wv_ref, cos_ref, sin_ref,
                     q_ref, k_ref, v_ref):
    m = m_ref[0].astype(F32)
    cq = m[:, :M_QL]
    ckv = m[:, M_QL:M_QL + M_KVL]
    kpa = m[:, M_QL + M_KVL:M_QL + M_KVL + LANE]
    kpb = m[:, M_QL + M_KVL + LANE:]
    hq = (cq * lax.rsqrt(jnp.mean(cq * cq, axis=-1, keepdims=True) + EPS) * qn_ref[...]).astype(BF)
    hkv = (ckv * lax.rsqrt(jnp.mean(ckv * ckv, axis=-1, keepdims=True) + EPS) * kvn_ref[...]).astype(BF)
    cos = cos_ref[...]
    sin = sin_ref[...]
    scale = (M_NOPE + M_ROPE) ** -0.5 * math.log2(math.e)
    q_all = jnp.dot(hq, wq_ref[...], preferred_element_type=F32)
    q_rot = jnp.dot(hq, wqr_ref[...], preferred_element_type=F32)
    k_all = jnp.dot(hkv, wk_ref[...], preferred_element_type=F32)
    v_all_t = _nt_dot(wv_ref[...], hkv)
    kpe = kpa * cos + kpb * sin
    rows = lax.broadcasted_iota(jnp.int32, (M_HP, hkv.shape[0]), 0)
    one_row = jnp.where(rows == M_V, 1.0, 0.0)
    for h in range(M_H):
        sl = slice(h * M_HP, (h + 1) * M_HP)
        q_ref[0, h] = ((q_all[:, sl] * cos + q_rot[:, sl] * sin) * scale).astype(q_ref.dtype)
        k_ref[0, h] = (k_all[:, sl] + kpe).astype(k_ref.dtype)
        v_ref[0, h] = (v_all_t[sl, :] + one_row).astype(v_ref.dtype)


def _mla_prep(mla_in, q_norm, kv_norm, wq, wqr, wk, wv, cos_t, sin_t):
    b, l, w = mla_in.shape
    tm = _tile(l, 512)
    full = lambda a: pl.BlockSpec(a.shape, lambda bi, i: (0,) * a.ndim)
    wvt = wv.T
    return pl.pallas_call(
        _mla_prep_kernel,
        grid=(b, l // tm),
        in_specs=[pl.BlockSpec((1, tm, w), lambda bi, i: (bi, i, 0)),
                  pl.BlockSpec((1, M_QL), lambda bi, i: (0, 0)),
                  pl.BlockSpec((1, M_KVL), lambda bi, i: (0, 0)),
                  full(wq), full(wqr), full(wk), full(wvt),
                  pl.BlockSpec((tm, LANE), lambda bi, i: (i, 0)),
                  pl.BlockSpec((tm, LANE), lambda bi, i: (i, 0))],
        out_specs=[pl.BlockSpec((1, M_H, tm, M_HP), lambda bi, i: (bi, 0, i, 0))] * 2
                  + [pl.BlockSpec((1, M_H, M_HP, tm), lambda bi, i: (bi, 0, 0, i))],
        out_shape=[jax.ShapeDtypeStruct((b, M_H, l, M_HP), BF)] * 2
                  + [jax.ShapeDtypeStruct((b, M_H, M_HP, l), BF)],
        compiler_params=_params(("parallel", "parallel")),
        name="mla_prep",
    )(mla_in, q_norm.reshape(1, M_QL).astype(F32), kv_norm.reshape(1, M_KVL).astype(F32),
      wq, wqr, wk, wvt, cos_t, sin_t)


def _attn_kernel(q_ref, *refs, n_src):
    k_refs, v_refs, o_ref = refs[:n_src], refs[n_src:2 * n_src], refs[2 * n_src]
    tq = q_ref.shape[2]
    sub = min(tq, ATT_SUB)
    units = [(hh, r0) for hh in range(2) for r0 in range(0, tq, sub)]

    tiles = [(src, k0, min(ATT_KT, k_r.shape[2] - k0))
             for src, k_r in enumerate(k_refs) for k0 in range(0, k_r.shape[2], ATT_KT)]

    def score_tile(u, tile):
        (hh, r0), (src, k0, ksz) = u, tile
        return _nt_dot(k_refs[src][0, hh, k0:k0 + ksz, :], q_ref[0, hh, r0:r0 + sub, :])

    def value_tile(u, tile, s_t, m):
        (hh, _), (src, k0, ksz) = u, tile
        return jnp.dot(v_refs[src][0, hh, :ATT_VROWS, k0:k0 + ksz], jnp.exp2(s_t - m).astype(BF),
                       preferred_element_type=F32)

    s_cur = [score_tile(units[0], t) for t in tiles]
    for i, u in enumerate(units):
        s_nxt = [score_tile(units[i + 1], t) for t in tiles] if i + 1 < len(units) else None
        m = s_cur[0].max(axis=0, keepdims=True)
        for s_t in s_cur[1:]:
            m = jnp.maximum(m, s_t.max(axis=0, keepdims=True))
        acc = None
        for t, s_t in zip(tiles, s_cur):
            part = value_tile(u, t, s_t, m)
            acc = part if acc is None else acc + part
        hh, r0 = u
        o = acc[:M_V] / acc[M_V:M_V + 1]
        o_ref[0, r0:r0 + sub, hh * M_V:(hh + 1) * M_V] = o.T.astype(o_ref.dtype)
        s_cur = s_nxt


def _attention(q, ks, vs):
    b, _, l, _ = q.shape
    tq = _tile(l, ATT_UNITS * ATT_SUB)
    n_src = len(ks)
    kv_spec = lambda a: pl.BlockSpec((1, 2) + a.shape[2:], lambda bi, hp, i: (bi, hp, 0, 0))
    return pl.pallas_call(
        functools.partial(_attn_kernel, n_src=n_src),
        grid=(b, M_H // 2, l // tq),
        in_specs=[pl.BlockSpec((1, 2, tq, M_HP), lambda bi, hp, i: (bi, hp, i, 0))]
                 + [kv_spec(a) for a in ks] + [kv_spec(a) for a in vs],
        out_specs=pl.BlockSpec((1, tq, 2 * M_V), lambda bi, hp, i: (bi, i, hp)),
        out_shape=jax.ShapeDtypeStruct((b, l, M_H * M_V), BF),
        compiler_params=_params(("parallel", "parallel", "arbitrary")),
        name="attention",
    )(q, *ks, *vs)


def _merge_kernel(pq_ref, faz_ref, of_ref, ob_ref, dnz_ref, om_ref, mz_ref, gl_ref, x_ref, gm_ref, gp_ref,
                  fw_ref, cc_ref, sc_ref, dnn_ref, wb_ref, wo_ref, o_ref):
    pq = pq_ref[0]
    faz = _silu(faz_ref[0]).astype(F32)
    ya = []
    for g in range(F_GROUPS):
        sl = slice(g * F_GD, (g + 1) * F_GD)
        spec = (jnp.dot(pq[:, sl], cc_ref[...], preferred_element_type=F32)
                - jnp.dot(pq[:, F_W + g * F_GD:F_W + (g + 1) * F_GD], sc_ref[...], preferred_element_type=F32))
        t = jnp.dot(spec.astype(BF), fw_ref[g], preferred_element_type=F32)
        ya.append((t * faz[:, sl]).astype(BF))
    ya = jnp.concatenate(ya, axis=1)
    osum = of_ref[0].astype(F32) + ob_ref[0].astype(F32)
    dnz = _silu(dnz_ref[0]).astype(F32)
    yb = []
    for h in range(DN_H):
        sl = slice(h * DN_DK, (h + 1) * DN_DK)
        seg = osum[:, sl]
        nrm = seg * lax.rsqrt(jnp.mean(seg * seg, axis=-1, keepdims=True) + EPS) * dnn_ref[...]
        yb.append((nrm * dnz[:, sl]).astype(BF))
    yb = jnp.concatenate(yb, axis=1)
    yc = om_ref[0] * _silu(mz_ref[0])
    merged = None
    for idx, yy in enumerate((ya, yb, yc)):
        gate = _sigmoid(gl_ref[0, :, idx * D:(idx + 1) * D]).astype(F32)
        term = gate * jnp.dot(yy, wb_ref[idx], preferred_element_type=F32)
        merged = term if merged is None else merged + term
    y = jnp.dot(merged.astype(BF), wo_ref[...], preferred_element_type=F32)
    yn = y * lax.rsqrt(jnp.mean(y * y, axis=-1, keepdims=True) + EPS) * gp_ref[...]
    o_ref[0] = x_ref[0] + gm_ref[0] * yn


def _merge(pq, fa_z, o_f, o_b, dn_z, o_m, mla_z, gate_logits, x, gate_mod, g_post, fw, cc, sc, dn_norm, wb, wo,
           per_batch_mod):
    b, l, _ = x.shape
    tm = _tile(l, 512)
    row = lambda w: pl.BlockSpec((1, tm, w), lambda bi, i: (bi, i, 0))
    full = lambda a: pl.BlockSpec(a.shape, lambda bi, i: (0,) * a.ndim)
    mod_map = (lambda bi, i: (bi, 0, 0)) if per_batch_mod else (lambda bi, i: (0, 0, 0))
    gp = g_post.reshape(1, D).astype(F32)
    dnn = dn_norm.reshape(1, DN_DK).astype(F32)
    return pl.pallas_call(
        _merge_kernel,
        grid=(b, l // tm),
        in_specs=[row(2 * F_W), row(F_W), row(DN_W), row(DN_W), row(DN_W), row(512), row(512), row(3 * D), row(D),
                  pl.BlockSpec((1, 1, D), mod_map), full(gp), full(fw), full(cc), full(sc), full(dnn),
                  full(wb), full(wo)],
        out_specs=row(D),
        out_shape=jax.ShapeDtypeStruct((b, l, D), F32),
        compiler_params=_params(("parallel", "parallel")),
        name="merge",
    )(pq, fa_z, o_f, o_b, dn_z, o_m, mla_z, gate_logits, x, gate_mod, gp, fw, cc, sc, dnn, wb, wo)


def _rot_cols(w):
    q = M_ROPE // 4
    return jnp.concatenate([-w[..., q:2 * q], w[..., :q], -w[..., 3 * q:], w[..., 2 * q:3 * q]], axis=-1)


def _layer_weights(w_in, mla_w_uq, mla_w_ukv):
    col = lambda n: w_in[:, _OFF[n][0]:_OFF[n][1]]
    zeros = lambda n: jnp.zeros((D, n), w_in.dtype)
    w_qkv = jnp.concatenate([col("dn_q"), col("dn_k"), col("dn_v")], axis=1)
    w_ab = jnp.concatenate([col("dn_ab"), zeros(LANE - 4 * DN_H)], axis=1)
    kpe = col("kpe")
    rest = LANE - M_NOPE - M_ROPE
    kpe_a = jnp.concatenate([zeros(M_NOPE), kpe, zeros(rest)], axis=1)
    kpe_b = jnp.concatenate([zeros(M_NOPE), _rot_cols(kpe), zeros(rest)], axis=1)
    w_mla = jnp.concatenate([col("cq"), col("ckv"), kpe_a, kpe_b], axis=1)
    grp1 = [col("fa_x"), col("fa_z"), w_qkv, col("dn_z"), w_ab]
    grp2 = [w_mla, col("mla_z"), col("gate")]
    uq = mla_w_uq.reshape(M_QL, M_H, M_NOPE + M_ROPE)
    zq = jnp.zeros((M_QL, M_H, rest), uq.dtype)
    wq = jnp.concatenate([uq, zq], axis=-1).reshape(M_QL, M_H * M_HP)
    wqr = jnp.concatenate([jnp.zeros((M_QL, M_H, M_NOPE), uq.dtype), _rot_cols(uq[..., M_NOPE:]), zq],
                          axis=-1).reshape(M_QL, M_H * M_HP)
    ukv = mla_w_ukv.reshape(M_KVL, M_H, M_NOPE + M_V)
    zk = jnp.zeros((M_KVL, M_H, M_HP - M_NOPE), ukv.dtype)
    wk = jnp.concatenate([ukv[..., :M_NOPE], zk], axis=-1).reshape(M_KVL, M_H * M_HP)
    wv = jnp.concatenate([ukv[..., M_NOPE:], jnp.zeros((M_KVL, M_H, M_HP - M_V), ukv.dtype)],
                         axis=-1).reshape(M_KVL, M_H * M_HP)
    cast = lambda ws: [w.astype(BF) for w in ws]
    return cast(grp1), cast(grp2), cast([wq, wqr, wk, wv])


def _rope_tables(l, rotary):
    ones = jnp.ones((l, M_NOPE), F32)
    pad = jnp.zeros((l, LANE - M_NOPE - M_ROPE), F32)
    if not rotary:
        return (jnp.concatenate([ones, jnp.ones((l, M_ROPE), F32), pad], axis=1),
                jnp.zeros((l, LANE), F32))
    pos = jnp.arange(l, dtype=jnp.int32)
    row = (pos // GRID_W).astype(F32)
    colp = (pos % GRID_W).astype(F32)
    n_freq = M_ROPE // 4
    inv = ROPE_BASE ** (-jnp.arange(n_freq, dtype=F32) / n_freq)
    ang_r = row[:, None] * inv
    ang_c = colp[:, None] * inv
    ang = jnp.concatenate([ang_r, ang_r, ang_c, ang_c], axis=-1)
    return (jnp.concatenate([ones, jnp.cos(ang), pad], axis=1),
            jnp.concatenate([jnp.zeros((l, M_NOPE), F32), jnp.sin(ang), pad], axis=1))


def _chan_tables():
    j = jnp.arange(F_GD, dtype=jnp.int32)
    a = ((j[:, None] * j[None, :]) % F_GD).astype(F32) * (2.0 * math.pi / F_GD)
    nrm = F_GD ** -0.5
    return (jnp.cos(a) * nrm).astype(BF), (jnp.sin(a) * nrm).astype(BF)


def _layer(x, ctx, mod, p, tables, need_ctx_out):
    b = x.shape[0]
    shift, scale, gate = (mod[:b, None, i * D:(i + 1) * D] for i in range(3))
    shift_c, scale_c, gate_c = (mod[b:b + 1, None, i * D:(i + 1) * D] for i in range(3))
    grp1, grp2, (wq, wqr, wk, wv) = _layer_weights(p["w_in"], p["mla_w_uq"], p["mla_w_ukv"])
    dts1 = [BF, BF, BF, BF, F32]
    dts2 = [BF, BF, BF]

    def project(t, sc, sh, per_batch):
        a = _inproj(t, sc, sh, p["g_pre"], grp1, dts1, per_batch, fold_first=_factorizable(t.shape[1]))
        bb = _inproj(t, sc, sh, p["g_pre"], grp2, dts2, per_batch)
        return a, bb

    (fa_x, fa_z, qkv, dn_z, ab), (mla_in, mla_z, gl) = project(x, scale, shift, True)
    (fa_xc, fa_zc, qkvc, dn_zc, abc), (mla_inc, mla_zc, glc) = project(ctx, scale_c, shift_c, False)

    gdn_args = (p["dn_conv"], p["dn_a_log"], p["dn_dt_bias"])
    s0 = jnp.zeros((b, 2, DN_PAIRS, DN_DK, DN_PW), F32)
    ofc, obc, s_ctx = _gdn_scan(*_gdn_local(qkvc, abc, *gdn_args), s0)
    of, ob, _ = _gdn_scan(*_gdn_local(qkv, ab, *gdn_args), s_ctx)

    q_m, k_m, v_m = _mla_prep(mla_in, p["mla_q_norm"], p["mla_kv_norm"], wq, wqr, wk, wv, *tables["rope"])
    qc_m, kc_m, vc_m = _mla_prep(mla_inc, p["mla_q_norm"], p["mla_kv_norm"], wq, wqr, wk, wv, *tables["rope_c"])
    o_m = _attention(q_m, [k_m, kc_m], [v_m, vc_m])

    pq = _position_dft(fa_x, tables["dft"])
    fw = p["f_w"].astype(BF)
    wb = p["w_branch"].astype(BF)
    wo = p["w_out"].astype(BF)
    cc, sc = tables["chan"]
    x_new = _merge(pq, fa_z, of, ob, dn_z, o_m, mla_z, gl, x, gate, p["g_post"], fw, cc, sc, p["dn_norm"],
                   wb, wo, True)
    ctx_new = ctx
    if need_ctx_out:
        pqc = _position_dft(fa_xc, tables["dft_c"])
        oc_m = _attention(qc_m, [kc_m], [vc_m])
        ctx_new = _merge(pqc, fa_zc, ofc, obc, dn_zc, oc_m, mla_zc, glc, ctx, gate_c, p["g_post"], fw, cc, sc,
                         p["dn_norm"], wb, wo, False)
    return x_new, ctx_new


def kernel(x, c, ctx, c_ctx, w_mod, b_mod, g_pre, g_post, w_in, f_w, dn_conv, dn_a_log, dn_dt_bias, dn_norm,
           mla_q_norm, mla_w_uq, mla_kv_norm, mla_w_ukv, w_branch, w_out):
    b, l, _ = x.shape
    lc = ctx.shape[1]
    depth = w_mod.shape[0]
    rows = -(-(b + 1) // 8) * 8
    c_all = jnp.concatenate([c, c_ctx[None, :], jnp.zeros((rows - b - 1, D), F32)], axis=0)
    tables = {"dft": _position_tables(l), "dft_c": _position_tables(lc), "chan": _chan_tables(),
              "rope": _rope_tables(l, True), "rope_c": _rope_tables(lc, False)}
    bf = lambda w: w.astype(BF)
    per_layer = dict(g_pre=g_pre, g_post=g_post, w_in=bf(w_in), f_w=bf(f_w), dn_conv=dn_conv, dn_a_log=dn_a_log,
                     dn_dt_bias=dn_dt_bias, dn_norm=dn_norm, mla_q_norm=mla_q_norm, mla_w_uq=bf(mla_w_uq),
                     mla_kv_norm=mla_kv_norm, mla_w_ukv=bf(mla_w_ukv), w_branch=bf(w_branch), w_out=bf(w_out))
    for li in range(depth):
        p = {k: v[li] for k, v in per_layer.items()}
        mod = _modulation(c_all, w_mod[li], b_mod[li])
        x, ctx = _layer(x, ctx, mod, p, tables, need_ctx_out=(li < depth - 1))
    return x
```
